```python
import math
import jax, jax.numpy as jnp
from jax import lax
import numpy as np

D_MODEL = 1024
BATCH = 8
SEQ = 4096
DEPTH = 4

CHUNK = 64
Q_BLOCK = 128
N_A_LAYERS = DEPTH // 2
N_B_LAYERS = DEPTH - N_A_LAYERS
EPS = 1e-6
NEG = -1e30

A_HEADS = 16
A_HEAD_DIM = D_MODEL // A_HEADS
IDX_HEADS = 8
IDX_DIM = 64
TOPK_MAX = 256
A_IN_COLS = A_HEADS * A_HEAD_DIM + 2 * A_HEAD_DIM + IDX_HEADS * IDX_DIM + IDX_DIM + IDX_HEADS

N_BUCKETS = 32
MAX_DISTANCE = 128

B_HEADS = 16
QK_NOPE = 64
QK_ROPE = 32
QK_DIM = QK_NOPE + QK_ROPE
V_DIM = 64
KV_LORA = 256
Q_LORA = 384
ROPE_THETA = 10000.0

D_FF = 4 * D_MODEL

kernel_name = "yoco_dsa_mla_hybrid_encoder"


def rms_norm(x, g):
    xf = x.astype(jnp.float32)
    y = xf * lax.rsqrt(jnp.mean(xf * xf, axis=-1, keepdims=True) + EPS)
    return (y * g.astype(jnp.float32)).astype(x.dtype)


def t5_bucket(rel):
    nb = N_BUCKETS // 2
    max_exact = nb // 2
    ret = jnp.where(rel > 0, nb, 0)
    n = jnp.abs(rel)
    nf = jnp.maximum(n, 1).astype(jnp.float32)
    large = max_exact + (jnp.log(nf / max_exact) / math.log(MAX_DISTANCE / max_exact)
                         * (nb - max_exact)).astype(jnp.int32)
    large = jnp.minimum(large, nb - 1)
    return ret + jnp.where(n < max_exact, n, large)


def to_blocks(a):
    b, s = a.shape[:2]
    return a.reshape(b, s // Q_BLOCK, Q_BLOCK, *a.shape[2:]).swapaxes(0, 1)


def from_blocks(a):
    nb, b, qb = a.shape[:3]
    return a.swapaxes(0, 1).reshape(b, nb * qb, *a.shape[3:])


def rope_tables(seq_len):
    pos = jnp.arange(seq_len, dtype=jnp.float32)
    inv_freq = 1.0 / (ROPE_THETA ** (jnp.arange(0, QK_ROPE, 2, dtype=jnp.float32) / QK_ROPE))
    ang = pos[:, None] * inv_freq[None, :]
    emb = jnp.concatenate([ang, ang], axis=-1)
    return jnp.cos(emb), jnp.sin(emb)


def rope_tail(x, cos, sin):
    xn, xr = x[..., :QK_NOPE], x[..., QK_NOPE:]
    half = QK_ROPE // 2
    rot = jnp.concatenate([-xr[..., half:], xr[..., :half]], axis=-1)
    xr = xr * cos[:, None, :] + rot * sin[:, None, :]
    return jnp.concatenate([xn, xr.astype(x.dtype)], axis=-1)


def dsa_mixer(h, w_in, q_gain, k_gain, w_o, rel_bias, n_top):
    b, s, _ = h.shape
    proj = h @ w_in
    o1 = A_HEADS * A_HEAD_DIM
    o2 = o1 + A_HEAD_DIM
    o3 = o2 + A_HEAD_DIM
    o4 = o3 + IDX_HEADS * IDX_DIM
    o5 = o4 + IDX_DIM
    q = rms_norm(proj[..., :o1].reshape(b, s, A_HEADS, A_HEAD_DIM), q_gain)
    k = rms_norm(proj[..., o1:o2], k_gain)
    v = proj[..., o2:o3]
    qi = proj[..., o3:o4].reshape(b, s, IDX_HEADS, IDX_DIM)
    ki = proj[..., o4:o5]
    wi = proj[..., o5:]
    key_chunk = jnp.arange(s) // CHUNK

    def block(args):
        qb, qib, wib, t0 = args
        t = t0 + jnp.arange(Q_BLOCK)
        t_chunk = t // CHUNK
        admiss = key_chunk[None, :] <= t_chunk[:, None]
        dots = jnp.einsum('bqhd,bsd->bqhs', qib, ki).astype(jnp.float32) * (IDX_DIM ** -0.5)
        score = jnp.einsum('bqh,bqhs->bqs', wib.astype(jnp.float32) * (IDX_HEADS ** -0.5),
                           jax.nn.relu(dots))
        score = jnp.where(admiss[None], score, NEG)
        _, idx = lax.top_k(score, n_top)
        valid = (idx // CHUNK) <= t_chunk[None, :, None]
        kg = jax.vmap(lambda kk, ii: kk[ii])(k, idx)
        vg = jax.vmap(lambda vv, ii: vv[ii])(v, idx)
        logits = jnp.einsum('bqhd,bqkd->bhqk', qb, kg).astype(jnp.float32) * (A_HEAD_DIM ** -0.5)
        bias = rel_bias[t5_bucket(idx - t[None, :, None])]
        logits = logits + jnp.transpose(bias, (0, 3, 1, 2)).astype(jnp.float32)
        logits = jnp.where(valid[:, None], logits, NEG)
        p = jax.nn.softmax(logits, axis=-1).astype(vg.dtype)
        return jnp.einsum('bhqk,bqkd->bqhd', p, vg)

    starts = jnp.arange(s // Q_BLOCK) * Q_BLOCK
    out = lax.map(block, (to_blocks(q), to_blocks(qi), to_blocks(wi), starts))
    out = from_blocks(out).reshape(b, s, A_HEADS * A_HEAD_DIM)
    return out @ w_o


def mla_shared_kv(h, w_dkv, kv_lora_gain, w_ukv, k_gain, cos, sin):
    b, s, _ = h.shape
    ckr = h @ w_dkv
    c_kv = rms_norm(ckr[..., :KV_LORA], kv_lora_gain)
    k_rope = ckr[..., KV_LORA:]
    kv = (c_kv @ w_ukv).reshape(b, s, B_HEADS, QK_NOPE + V_DIM)
    k_nope, v = kv[..., :QK_NOPE], kv[..., QK_NOPE:]
    k_rope_h = jnp.broadcast_to(k_rope[:, :, None, :], (b, s, B_HEADS, QK_ROPE))
    k = rms_norm(jnp.concatenate([k_nope, k_rope_h], axis=-1), k_gain)
    return rope_tail(k, cos, sin), v


def mla_mixer(h, w_dq, q_lora_gain, w_uq, q_gain, w_o, k, v, cos, sin):
    b, s, _ = h.shape
    q = (rms_norm(h @ w_dq, q_lora_gain) @ w_uq).reshape(b, s, B_HEADS, QK_DIM)
    q = rope_tail(rms_norm(q, q_gain), cos, sin)
    key_chunk = jnp.arange(s) // CHUNK

    def block(args):
        qb, t0 = args
        t = t0 + jnp.arange(Q_BLOCK)
        mask = key_chunk[None, :] <= (t // CHUNK)[:, None]
        logits = jnp.einsum('bqhd,bshd->bhqs', qb, k).astype(jnp.float32) * (QK_DIM ** -0.5)
        logits = jnp.where(mask, logits, NEG)
        p = jax.nn.softmax(logits, axis=-1).astype(v.dtype)
        return jnp.einsum('bhqs,bshd->bqhd', p, v)

    starts = jnp.arange(s // Q_BLOCK) * Q_BLOCK
    out = lax.map(block, (to_blocks(q), starts))
    out = from_blocks(out).reshape(b, s, B_HEADS * V_DIM)
    return out @ w_o


def sq_relu_mlp(h, w_up, w_down):
    return jnp.square(jax.nn.relu(h @ w_up)) @ w_down


def setup_inputs(seed: int = 0) -> dict:
    key = jax.random.key(seed)
    ks = iter(jax.random.split(key, 32))

    def w(shape, fan_in):
        return jax.random.normal(next(ks), shape, jnp.float32) * (fan_in ** -0.5)

    def g(shape):
        return 1.0 + 0.02 * jax.random.normal(next(ks), shape, jnp.float32)

    return {
        "x": jax.random.normal(next(ks), (BATCH, SEQ, D_MODEL), jnp.float32),
        "rel_bias": 0.5 * jax.random.normal(next(ks), (N_BUCKETS, A_HEADS), jnp.float32),
        "a_attn_norm": g((N_A_LAYERS, D_MODEL)),
        "a_w_in": w((N_A_LAYERS, D_MODEL, A_IN_COLS), D_MODEL),
        "a_q_norm": g((N_A_LAYERS, A_HEAD_DIM)),
        "a_k_norm": g((N_A_LAYERS, A_HEAD_DIM)),
        "a_w_o": w((N_A_LAYERS, A_HEADS * A_HEAD_DIM, D_MODEL), A_HEADS * A_HEAD_DIM),
        "kv_norm": g((D_MODEL,)),
        "w_dkv": w((D_MODEL, KV_LORA + QK_ROPE), D_MODEL),
        "kv_lora_norm": g((KV_LORA,)),
        "w_ukv": w((KV_LORA, B_HEADS * (QK_NOPE + V_DIM)), KV_LORA),
        "k_norm": g((QK_DIM,)),
        "b_attn_norm": g((N_B_LAYERS, D_MODEL)),
        "b_w_dq": w((N_B_LAYERS, D_MODEL, Q_LORA), D_MODEL),
        "b_q_lora_norm": g((N_B_LAYERS, Q_LORA)),
        "b_w_uq": w((N_B_LAYERS, Q_LORA, B_HEADS * QK_DIM), Q_LORA),
        "b_q_norm": g((N_B_LAYERS, QK_DIM)),
        "b_w_o": w((N_B_LAYERS, B_HEADS * V_DIM, D_MODEL), B_HEADS * V_DIM),
        "mlp_norm": g((DEPTH, D_MODEL)),
        "mlp_w_up": w((DEPTH, D_MODEL, D_FF), D_MODEL),
        "mlp_w_down": w((DEPTH, D_FF, D_MODEL), D_FF),
    }


def reference(x, rel_bias, a_attn_norm, a_w_in, a_q_norm, a_k_norm, a_w_o,
              kv_norm, w_dkv, kv_lora_norm, w_ukv, k_norm,
              b_attn_norm, b_w_dq, b_q_lora_norm, b_w_uq, b_q_norm, b_w_o,
              mlp_norm, mlp_w_up, mlp_w_down):
    s = x.shape[1]
    n_top = min(TOPK_MAX, s // 4)
    cos, sin = rope_tables(s)
    h = x
    k_shared = None
    v_shared = None
    for layer in range(DEPTH):
        if layer < N_A_LAYERS:
            i = layer
            h = h + dsa_mixer(rms_norm(h, a_attn_norm[i]), a_w_in[i], a_q_norm[i], a_k_norm[i],
                              a_w_o[i], rel_bias, n_top)
        else:
            if layer == N_A_LAYERS:
                k_shared, v_shared = mla_shared_kv(rms_norm(h, kv_norm), w_dkv, kv_lora_norm,
                                                   w_ukv, k_norm, cos, sin)
            j = layer - N_A_LAYERS
            h = h + mla_mixer(rms_norm(h, b_attn_norm[j]), b_w_dq[j], b_q_lora_norm[j], b_w_uq[j],
                              b_q_norm[j], b_w_o[j], k_shared, v_shared, cos, sin)
        h = h + sq_relu_mlp(rms_norm(h, mlp_norm[layer]), mlp_w_up[layer], mlp_w_down[layer])
    return h
```

```python
import functools
import math

import numpy as np
import jax
import jax.numpy as jnp
from jax import lax
from jax.experimental import pallas as pl
from jax.experimental.pallas import tpu as pltpu

D_MODEL = 1024
CHUNK = 64
CHUNK_SHIFT = 6
EPS = 1e-6
NEG = -1e30
LOG2E = 1.4426950408889634

A_HEADS = 16
A_HEAD_DIM = 64
IDX_HEADS = 8
IDX_DIM = 64
TOPK_MAX = 256
N_BUCKETS = 32
MAX_DISTANCE = 128

B_HEADS = 16
QK_NOPE = 64
QK_ROPE = 32
QK_DIM = QK_NOPE + QK_ROPE
V_DIM = 64
KV_LORA = 256
Q_LORA = 384
ROPE_THETA = 10000.0
D_FF = 4 * D_MODEL

LANES = 128
INT_MIN = -(2 ** 31)
VMEM_LIMIT = 56 * 1024 * 1024

ROW_TILE = 512
DSA_TQ = 128
DSA_TK = 256
SEL_RB = 256
MLA_TQ = 512
MLA_TK = 256

_NT = (((1,), (1,)), ((), ()))


def _cparams(sem):
    return pltpu.CompilerParams(dimension_semantics=sem, vmem_limit_bytes=VMEM_LIMIT)


def _rms(x, g):
    return x * lax.rsqrt(jnp.mean(x * x, axis=-1, keepdims=True) + EPS) * g


def _rms_padded(x, g, true_dim):
    ms = jnp.sum(x * x, axis=-1, keepdims=True) * (1.0 / true_dim)
    return x * lax.rsqrt(ms + EPS) * g


def _dsa_proj_kernel(h_ref, g_ref, wq_ref, wr_ref, gq_ref, gk_ref, cb_ref, q_ref, kv_ref, qi_ref, misc_ref):
    xn = _rms(h_ref[0], g_ref[...]).astype(jnp.bfloat16)
    yq = jnp.dot(xn, wq_ref[...], preferred_element_type=jnp.float32)
    for h in range(A_HEADS):
        qh = yq[:, h * LANES:(h + 1) * LANES]
        qn = _rms_padded(qh, gq_ref[...], A_HEAD_DIM) + cb_ref[h:h + 1, :]
        q_ref[0, h] = qn.astype(jnp.bfloat16)
    yr = jnp.dot(xn, wr_ref[...], preferred_element_type=jnp.float32)
    lane = lax.broadcasted_iota(jnp.int32, (1, LANES), 1)
    k_ones = jnp.where((lane == A_HEAD_DIM) | (lane == A_HEAD_DIM + 1), 1.0, 0.0)
    v_ones = jnp.where(lane == A_HEAD_DIM, 1.0, 0.0)
    kn = _rms_padded(yr[:, :LANES], gk_ref[...], A_HEAD_DIM) + k_ones
    vv = yr[:, LANES:2 * LANES] + v_ones
    kv_ref[0, :, :LANES] = kn.astype(jnp.bfloat16)
    kv_ref[0, :, LANES:] = vv.astype(jnp.bfloat16)
    qi = yr[:, 2 * LANES:2 * LANES + IDX_HEADS * IDX_DIM] * (IDX_DIM ** -0.5)
    for h in range(IDX_HEADS):
        qi_ref[0, h] = qi[:, h * IDX_DIM:(h + 1) * IDX_DIM].astype(jnp.bfloat16)
    misc_ref[0] = yr[:, 2 * LANES + IDX_HEADS * IDX_DIM:]


def _dsa_proj(h, g, wq, wr, gq, gk, cb):
    b, s, d = h.shape
    tm = min(ROW_TILE, s)
    nr = wr.shape[1]
    const = lambda shape: pl.BlockSpec(shape, lambda i, j: (0,) * len(shape))
    return pl.pallas_call(
        _dsa_proj_kernel,
        grid=(b, s // tm),
        in_specs=[
            pl.BlockSpec((1, tm, d), lambda i, j: (i, j, 0)),
            const((1, d)), const(wq.shape), const(wr.shape), const((1, LANES)), const((1, LANES)),
            const((A_HEADS, LANES)),
        ],
        out_specs=[
            pl.BlockSpec((1, A_HEADS, tm, LANES), lambda i, j: (i, 0, j, 0)),
            pl.BlockSpec((1, tm, 2 * LANES), lambda i, j: (i, j, 0)),
            pl.BlockSpec((1, IDX_HEADS, tm, IDX_DIM), lambda i, j: (i, 0, j, 0)),
            pl.BlockSpec((1, tm, LANES), lambda i, j: (i, j, 0)),
        ],
        out_shape=[
            jax.ShapeDtypeStruct((b, A_HEADS, s, LANES), jnp.bfloat16),
            jax.ShapeDtypeStruct((b, s, 2 * LANES), jnp.bfloat16),
            jax.ShapeDtypeStruct((b, IDX_HEADS, s, IDX_DIM), jnp.bfloat16),
            jax.ShapeDtypeStruct((b, s, LANES), jnp.float32),
        ],
        compiler_params=_cparams(("parallel", "parallel")),
    )(h, g, wq, wr, gq, gk, cb)


def _dsa_select_kernel(ki_ref, qi_ref, wi_ref, mask_ref, keys_scr, *, n_top, seq):
    rb = SEL_RB
    t0 = pl.program_id(1) * DSA_TQ
    nblk = (t0 + DSA_TQ + rb - 1) // rb
    qi2d = qi_ref[0].reshape(IDX_HEADS * DSA_TQ, IDX_DIM)
    w = wi_ref[0]
    t_chunk = (t0 + lax.broadcasted_iota(jnp.int32, (1, DSA_TQ), 1)) >> CHUNK_SHIFT
    row_iota = lax.broadcasted_iota(jnp.int32, (rb, DSA_TQ), 0)

    def admissible(r0):
        return ((r0 + row_iota) >> CHUNK_SHIFT) <= t_chunk

    def score_blk(blk, carry):
        r0 = pl.multiple_of(blk * rb, rb)
        d = lax.dot_general(ki_ref[0, pl.ds(r0, rb), :], qi2d, _NT, preferred_element_type=jnp.float32)
        acc = jnp.zeros((rb, DSA_TQ), jnp.float32)
        for h in range(IDX_HEADS):
            acc = acc + w[h:h + 1, :] * jnp.maximum(d[:, h * DSA_TQ:(h + 1) * DSA_TQ], 0.0)
        bits = lax.bitcast_convert_type(acc, jnp.int32)
        key = bits ^ ((bits >> 31) & 0x7FFFFFFF)
        keys_scr[pl.ds(r0, rb), :] = jnp.where(admissible(r0), key, INT_MIN)
        return carry

    lax.fori_loop(0, nblk, score_blk, 0)

    def count(pred_fn):
        def blk_body(blk, cnt):
            r0 = pl.multiple_of(blk * rb, rb)
            hit = jnp.where(pred_fn(keys_scr[pl.ds(r0, rb), :]), 1, 0)
            return cnt + jnp.sum(hit.reshape(rb // 8, 8, DSA_TQ), axis=0)
        cnt8 = lax.fori_loop(0, nblk, blk_body, jnp.zeros((8, DSA_TQ), jnp.int32))
        return jnp.sum(cnt8, axis=0, keepdims=True)

    def bit_pass(i, thr):
        trial = thr + lax.shift_left(jnp.int32(1), 31 - i)
        cnt = count(lambda k: k >= trial)
        return jnp.where(cnt >= n_top, trial, thr)

    thr = lax.fori_loop(0, 32, bit_pass, jnp.full((1, DSA_TQ), INT_MIN, jnp.int32))
    quota = (n_top - count(lambda k: k > thr)).astype(jnp.float32)

    tri = (lax.broadcasted_iota(jnp.int32, (rb, rb), 1)
           < lax.broadcasted_iota(jnp.int32, (rb, rb), 0))
    tri = jnp.where(tri, 1.0, 0.0).astype(jnp.bfloat16)

    def emit_blk(blk, seen):
        r0 = pl.multiple_of(blk * rb, rb)
        key = keys_scr[pl.ds(r0, rb), :]
        eq = key == thr
        eqf = jnp.where(eq, 1.0, 0.0)
        before = jnp.dot(tri, eqf.astype(jnp.bfloat16), preferred_element_type=jnp.float32) + seen
        sel = (key > thr) | (eq & (before < quota))
        sel = sel & admissible(r0)
        mask_ref[0, pl.ds(r0, rb), :] = jnp.where(sel, 0.0, NEG).astype(jnp.bfloat16)
        return seen + jnp.sum(eqf, axis=0, keepdims=True)

    lax.fori_loop(0, nblk, emit_blk, jnp.zeros((1, DSA_TQ), jnp.float32))

    def fill_blk(blk, carry):
        r0 = pl.multiple_of(blk * rb, rb)
        mask_ref[0, pl.ds(r0, rb), :] = jnp.full((rb, DSA_TQ), NEG, jnp.bfloat16)
        return carry

    lax.fori_loop(nblk, seq // rb, fill_blk, 0)


def _dsa_select(ki, qi, wi_t, n_top):
    b, s, _ = ki.shape
    return pl.pallas_call(
        functools.partial(_dsa_select_kernel, n_top=n_top, seq=s),
        grid=(b, s // DSA_TQ),
        in_specs=[
            pl.BlockSpec((1, s, IDX_DIM), lambda i, j: (i, 0, 0)),
            pl.BlockSpec((1, IDX_HEADS, DSA_TQ, IDX_DIM), lambda i, j: (i, 0, j, 0)),
            pl.BlockSpec((1, IDX_HEADS, DSA_TQ), lambda i, j: (i, 0, j)),
        ],
        out_specs=pl.BlockSpec((1, s, DSA_TQ), lambda i, j: (i, 0, j)),
        out_shape=jax.ShapeDtypeStruct((b, s, s), jnp.bfloat16),
        scratch_shapes=[pltpu.VMEM((s, DSA_TQ), jnp.int32)],
        compiler_params=_cparams(("parallel", "parallel")),
    )(ki, qi, wi_t)


def _dsa_attn_kernel(q_ref, kv_ref, mask_ref, bres_ref, o_ref, lhs_scr, s_scr, p_scr, m_scr, acc_scr):
    rows = A_HEADS * DSA_TQ
    qidx = pl.program_id(1)
    nk = (qidx * DSA_TQ + DSA_TQ + DSA_TK - 1) // DSA_TK
    odd = qidx % 2

    lhs_scr[:, :LANES] = q_ref[0].reshape(rows, LANES)
    r = lax.broadcasted_iota(jnp.int32, (rows, DSA_TQ), 0)
    c = lax.broadcasted_iota(jnp.int32, (rows, DSA_TQ), 1)
    lhs_scr[:, LANES:] = jnp.where((r & (DSA_TQ - 1)) == c, 1.0, 0.0).astype(jnp.bfloat16)
    m_scr[...] = jnp.full(m_scr.shape, -jnp.inf, jnp.float32)
    acc_scr[...] = jnp.zeros(acc_scr.shape, jnp.float32)

    def tile_body(j, carry):
        k0 = pl.multiple_of(j * DSA_TK, DSA_TK)
        rhs = jnp.concatenate([kv_ref[0, pl.ds(k0, DSA_TK), :LANES], mask_ref[0, pl.ds(k0, DSA_TK), :]], axis=1)
        s_scr[...] = lax.dot_general(lhs_scr[...], rhs, _NT, preferred_element_type=jnp.float32)
        cfg = jnp.where(j == nk - 1, odd, jnp.where((j == nk - 2) & (odd == 0), 2, 3))

        def head_body(h, carry2):
            r0 = pl.multiple_of(h * DSA_TQ, DSA_TQ)
            s = s_scr[pl.ds(r0, DSA_TQ), :] + bres_ref[cfg, pl.ds(r0, DSA_TQ), :]
            m_old = m_scr[pl.ds(r0, DSA_TQ), :]
            m_new = jnp.maximum(m_old, jnp.max(s, axis=1, keepdims=True))
            alpha = jnp.exp2(m_old - m_new)
            p = jnp.exp2((s - jnp.concatenate([m_new, m_new], axis=1)).astype(jnp.bfloat16))
            p_scr[pl.ds(r0, DSA_TQ), :] = p
            m_scr[pl.ds(r0, DSA_TQ), :] = m_new
            acc_scr[pl.ds(r0, DSA_TQ), :] = acc_scr[pl.ds(r0, DSA_TQ), :] * alpha
            return carry2

        lax.fori_loop(0, A_HEADS, head_body, 0)
        acc_scr[...] += jnp.dot(p_scr[...], kv_ref[0, pl.ds(k0, DSA_TK), LANES:],
                                preferred_element_type=jnp.float32)
        return carry

    lax.fori_loop(0, nk, tile_body, 0)

    for h in range(A_HEADS):
        a = acc_scr[h * DSA_TQ:(h + 1) * DSA_TQ, :]
        o = a[:, :A_HEAD_DIM] / a[:, A_HEAD_DIM:A_HEAD_DIM + 1]
        o_ref[0, :, h * A_HEAD_DIM:(h + 1) * A_HEAD_DIM] = o.astype(jnp.bfloat16)


def _dsa_attn(q_pad, kv, mask_t, bres):
    b, _, s, _ = q_pad.shape
    rows = A_HEADS * DSA_TQ
    return pl.pallas_call(
        _dsa_attn_kernel,
        grid=(b, s // DSA_TQ),
        in_specs=[
            pl.BlockSpec((1, A_HEADS, DSA_TQ, LANES), lambda i, j: (i, 0, j, 0)),
            pl.BlockSpec((1, s, 2 * LANES), lambda i, j: (i, 0, 0)),
            pl.BlockSpec((1, s, DSA_TQ), lambda i, j: (i, 0, j)),
            pl.BlockSpec(bres.shape, lambda i, j: (0, 0, 0)),
        ],
        out_specs=pl.BlockSpec((1, DSA_TQ, A_HEADS * A_HEAD_DIM), lambda i, j: (i, j, 0)),
        out_shape=jax.ShapeDtypeStruct((b, s, A_HEADS * A_HEAD_DIM), jnp.bfloat16),
        scratch_shapes=[
            pltpu.VMEM((rows, 2 * LANES), jnp.bfloat16),
            pltpu.VMEM((rows, DSA_TK), jnp.float32),
            pltpu.VMEM((rows, DSA_TK), jnp.bfloat16),
            pltpu.VMEM((rows, LANES), jnp.float32),
            pltpu.VMEM((rows, LANES), jnp.float32),
        ],
        compiler_params=_cparams(("parallel", "parallel")),
    )(q_pad, kv, mask_t, bres)


def _out_proj_kernel(h_ref, a_ref, w_ref, o_ref):
    o_ref[...] = h_ref[...] + jnp.dot(a_ref[...], w_ref[...], preferred_element_type=jnp.float32)


def _out_proj_residual(h2d, a2d, w):
    n, d = h2d.shape
    tm = min(ROW_TILE, n)
    return pl.pallas_call(
        _out_proj_kernel,
        grid=(n // tm,),
        in_specs=[
            pl.BlockSpec((tm, d), lambda i: (i, 0)),
            pl.BlockSpec((tm, a2d.shape[1]), lambda i: (i, 0)),
            pl.BlockSpec(w.shape, lambda i: (0, 0)),
        ],
        out_specs=pl.BlockSpec((tm, d), lambda i: (i, 0)),
        out_shape=jax.ShapeDtypeStruct((n, d), jnp.float32),
        compiler_params=_cparams(("parallel",)),
    )(h2d, a2d, w)


def _mlp_kernel(h_ref, g_ref, wu_ref, wd_ref, o_ref, *, tf):
    x = h_ref[...]
    xn = _rms(x, g_ref[...]).astype(jnp.bfloat16)
    o_ref[...] = x
    for f in range(0, D_FF, tf):
        u = jnp.dot(xn, wu_ref[:, f:f + tf], preferred_element_type=jnp.float32)
        a = jnp.square(jnp.maximum(u, 0.0)).astype(jnp.bfloat16)
        o_ref[...] += jnp.dot(a, wd_ref[f:f + tf, :], preferred_element_type=jnp.float32)


def _mlp(h2d, g, wu, wd):
    n, d = h2d.shape
    tm = min(ROW_TILE, n)
    resident = lambda shape: pl.BlockSpec(shape, lambda i: (0, 0), pipeline_mode=pl.Buffered(1))
    return pl.pallas_call(
        functools.partial(_mlp_kernel, tf=1024),
        grid=(n // tm,),
        in_specs=[
            pl.BlockSpec((tm, d), lambda i: (i, 0)),
            pl.BlockSpec((1, d), lambda i: (0, 0)),
            resident(wu.shape), resident(wd.shape),
        ],
        out_specs=pl.BlockSpec((tm, d), lambda i: (i, 0)),
        out_shape=jax.ShapeDtypeStruct((n, d), jnp.float32),
        compiler_params=_cparams(("parallel",)),
    )(h2d, g, wu, wd)


def _rope_padded(x, cos_f, sin_a, sin_b):
    return x * cos_f + pltpu.roll(x, LANES - QK_ROPE // 2, 1) * sin_a + pltpu.roll(x, QK_ROPE // 2, 1) * sin_b


def _mla_kv_kernel(h_ref, g_ref, wc_ref, wr_ref, gl_ref, wk_ref, wv_ref, gk_ref, cos_ref, sa_ref, sb_ref,
                   k_ref, v_ref):
    xn = _rms(h_ref[0], g_ref[...]).astype(jnp.bfloat16)
    c = jnp.dot(xn, wc_ref[...], preferred_element_type=jnp.float32)
    c = _rms(c, gl_ref[...]).astype(jnp.bfloat16)
    k_rope = jnp.dot(xn, wr_ref[...], preferred_element_type=jnp.float32)
    kn = jnp.dot(c, wk_ref[...], preferred_element_type=jnp.float32)
    vv = jnp.dot(c, wv_ref[...], preferred_element_type=jnp.float32)
    lane = lax.broadcasted_iota(jnp.int32, (1, LANES), 1)
    v_ones = jnp.where(lane == V_DIM, 1.0, 0.0)
    cos_f, sin_a, sin_b = cos_ref[...], sa_ref[...], sb_ref[...]
    for h in range(B_HEADS):
        kh = kn[:, h * LANES:(h + 1) * LANES] + k_rope
        kh = _rope_padded(_rms_padded(kh, gk_ref[...], QK_DIM), cos_f, sin_a, sin_b)
        k_ref[0, h] = kh.astype(jnp.bfloat16)
        v_ref[0, h] = (vv[:, h * LANES:(h + 1) * LANES] + v_ones).astype(jnp.bfloat16)


def _mla_kv_prep(h, g, wc, wr, gl, wk, wv, gk, cos_f, sin_a, sin_b):
    b, s, d = h.shape
    tm = min(ROW_TILE, s)
    const = lambda shape: pl.BlockSpec(shape, lambda i, j: (0,) * len(shape))
    pos = pl.BlockSpec((tm, LANES), lambda i, j: (j, 0))
    head_out = pl.BlockSpec((1, B_HEADS, tm, LANES), lambda i, j: (i, 0, j, 0))
    return pl.pallas_call(
        _mla_kv_kernel,
        grid=(b, s // tm),
        in_specs=[
            pl.BlockSpec((1, tm, d), lambda i, j: (i, j, 0)),
            const((1, d)), const(wc.shape), const(wr.shape), const((1, KV_LORA)), const(wk.shape),
            const(wv.shape), const((1, LANES)), pos, pos, pos,
        ],
        out_specs=[head_out, head_out],
        out_shape=[jax.ShapeDtypeStruct((b, B_HEADS, s, LANES), jnp.bfloat16)] * 2,
        compiler_params=_cparams(("parallel", "parallel")),
    )(h, g, wc, wr, gl, wk, wv, gk, cos_f, sin_a, sin_b)


def _mla_q_kernel(h_ref, g_ref, wd_ref, gl_ref, wu_ref, gq_ref, cos_ref, sa_ref, sb_ref, q_ref):
    xn = _rms(h_ref[0], g_ref[...]).astype(jnp.bfloat16)
    c = jnp.dot(xn, wd_ref[...], preferred_element_type=jnp.float32)
    c = _rms(c, gl_ref[...]).astype(jnp.bfloat16)
    yq = jnp.dot(c, wu_ref[...], preferred_element_type=jnp.float32)
    cos_f, sin_a, sin_b = cos_ref[...], sa_ref[...], sb_ref[...]
    for h in range(B_HEADS):
        qh = _rms_padded(yq[:, h * LANES:(h + 1) * LANES], gq_ref[...], QK_DIM)
        q_ref[0, h] = _rope_padded(qh, cos_f, sin_a, sin_b).astype(jnp.bfloat16)


def _mla_q_prep(h, g, wd, gl, wu, gq, cos_f, sin_a, sin_b):
    b, s, d = h.shape
    tm = min(ROW_TILE, s)
    const = lambda shape: pl.BlockSpec(shape, lambda i, j: (0,) * len(shape))
    pos = pl.BlockSpec((tm, LANES), lambda i, j: (j, 0))
    return pl.pallas_call(
        _mla_q_kernel,
        grid=(b, s // tm),
        in_specs=[
            pl.BlockSpec((1, tm, d), lambda i, j: (i, j, 0)),
            const((1, d)), const(wd.shape), const((1, Q_LORA)), const(wu.shape), const((1, LANES)),
            pos, pos, pos,
        ],
        out_specs=pl.BlockSpec((1, B_HEADS, tm, LANES), lambda i, j: (i, 0, j, 0)),
        out_shape=jax.ShapeDtypeStruct((b, B_HEADS, s, LANES), jnp.bfloat16),
        compiler_params=_cparams(("parallel", "parallel")),
    )(h, g, wd, gl, wu, gq, cos_f, sin_a, sin_b)


def _mla_attn_kernel(q_ref, k_ref, v_ref, o_ref, m_scr, acc_scr, *, tq):
    t0 = pl.program_id(2) * tq
    nk = (t0 + tq + MLA_TK - 1) // MLA_TK
    q_chunk = (t0 + lax.broadcasted_iota(jnp.int32, (tq, MLA_TK), 0)) >> CHUNK_SHIFT
    k_iota = lax.broadcasted_iota(jnp.int32, (tq, MLA_TK), 1)
    outs = []
    for hh in range(2):
        m_scr[...] = jnp.full(m_scr.shape, -jnp.inf, jnp.float32)
        acc_scr[...] = jnp.zeros(acc_scr.shape, jnp.float32)
        q = q_ref[0, hh]

        def tile_body(j, carry):
            k0 = pl.multiple_of(j * MLA_TK, MLA_TK)
            s = lax.dot_general(q, k_ref[0, hh, pl.ds(k0, MLA_TK), :], _NT, preferred_element_type=jnp.float32)
            s = jnp.where(((k0 + k_iota) >> CHUNK_SHIFT) <= q_chunk, s, NEG)
            m_old = m_scr[...]
            m_new = jnp.maximum(m_old, jnp.max(s, axis=1, keepdims=True))
            alpha = jnp.exp2(m_old - m_new)
            p = jnp.exp2((s - jnp.concatenate([m_new, m_new], axis=1)).astype(jnp.bfloat16))
            pv = jnp.dot(p, v_ref[0, hh, pl.ds(k0, MLA_TK), :], preferred_element_type=jnp.float32)
            acc_scr[...] = acc_scr[...] * alpha + pv
            m_scr[...] = m_new
            return carry

        lax.fori_loop(0, nk, tile_body, 0)
        a = acc_scr[...]
        outs.append(a[:, :V_DIM] / a[:, V_DIM:V_DIM + 1])
    o_ref[0] = jnp.concatenate(outs, axis=1).astype(jnp.bfloat16)


def _mla_attn(q_pad, k_pad, v_aug):
    b, nh, s, _ = q_pad.shape
    tq = min(MLA_TQ, s)
    return pl.pallas_call(
        functools.partial(_mla_attn_kernel, tq=tq),
        grid=(b, nh // 2, s // tq),
        in_specs=[
            pl.BlockSpec((1, 2, tq, LANES), lambda i, p, j: (i, p, j, 0)),
            pl.BlockSpec((1, 2, s, LANES), lambda i, p, j: (i, p, 0, 0)),
            pl.BlockSpec((1, 2, s, LANES), lambda i, p, j: (i, p, 0, 0)),
        ],
        out_specs=pl.BlockSpec((1, tq, 2 * V_DIM), lambda i, p, j: (i, j, p)),
        out_shape=jax.ShapeDtypeStruct((b, s, nh * V_DIM), jnp.bfloat16),
        scratch_shapes=[pltpu.VMEM((tq, LANES), jnp.float32), pltpu.VMEM((tq, LANES), jnp.float32)],
        compiler_params=_cparams(("parallel", "parallel", "parallel")),
    )(q_pad, k_pad, v_aug)


def _t5_bucket(rel):
    nb = N_BUCKETS // 2
    max_exact = nb // 2
    ret = jnp.where(rel > 0, nb, 0)
    n = jnp.abs(rel)
    nf = jnp.maximum(n, 1).astype(jnp.float32)
    large = max_exact + (jnp.log(nf / max_exact) / math.log(MAX_DISTANCE / max_exact)
                         * (nb - max_exact)).astype(jnp.int32)
    large = jnp.minimum(large, nb - 1)
    return ret + jnp.where(n < max_exact, n, large)


def _pad_lanes(a, width=LANES):
    return jnp.pad(a, [(0, 0)] * (a.ndim - 1) + [(0, width - a.shape[-1])])


def _pad_heads(w, n_heads, head_dim):
    k = w.shape[0]
    return _pad_lanes(w.reshape(k, n_heads, head_dim)).reshape(k, n_heads * LANES)


def _bias_tables(rel_bias):
    far = rel_bias[N_BUCKETS // 2 - 1].astype(jnp.float32) * LOG2E
    c_hi = far.astype(jnp.bfloat16)
    c_lo = (far - c_hi.astype(jnp.float32)).astype(jnp.bfloat16)
    far_eff = c_hi.astype(jnp.float32) + c_lo.astype(jnp.float32)
    cb = jnp.zeros((A_HEADS, LANES), jnp.float32)
    cb = cb.at[:, A_HEAD_DIM].set(c_hi.astype(jnp.float32)).at[:, A_HEAD_DIM + 1].set(c_lo.astype(jnp.float32))
    r = jnp.arange(DSA_TQ, dtype=jnp.int32)[:, None]
    c = jnp.arange(DSA_TK, dtype=jnp.int32)[None, :]
    tiles = []
    for off in (0, DSA_TQ, DSA_TK):
        rel = c - r - off
        bias = rel_bias[_t5_bucket(rel)].astype(jnp.float32) * LOG2E
        tiles.append(jnp.transpose(bias, (2, 0, 1)) - far_eff[:, None, None])
    tiles.append(jnp.zeros_like(tiles[0]))
    bres = jnp.stack(tiles).reshape(4, A_HEADS * DSA_TQ, DSA_TK)
    return cb, bres


def _rope_tables_padded(seq_len):
    pos = jnp.arange(seq_len, dtype=jnp.float32)
    inv_freq = 1.0 / (ROPE_THETA ** (jnp.arange(0, QK_ROPE, 2, dtype=jnp.float32) / QK_ROPE))
    ang = pos[:, None] * inv_freq[None, :]
    cos, sin = jnp.cos(ang), jnp.sin(ang)
    half = QK_ROPE // 2
    ones = jnp.ones((seq_len, QK_NOPE), jnp.float32)
    zeros = jnp.zeros((seq_len, QK_NOPE), jnp.float32)
    zh = jnp.zeros((seq_len, half), jnp.float32)
    cos_f = _pad_lanes(jnp.concatenate([ones, cos, cos], axis=1))
    sin_a = _pad_lanes(jnp.concatenate([zeros, -sin, zh], axis=1))
    sin_b = _pad_lanes(jnp.concatenate([zeros, zh, sin], axis=1))
    return cos_f, sin_a, sin_b


def kernel(x, rel_bias, a_attn_norm, a_w_in, a_q_norm, a_k_norm, a_w_o, kv_norm, w_dkv, kv_lora_norm, w_ukv, k_norm, b_attn_norm, b_w_dq, b_q_lora_norm, b_w_uq, b_q_norm, b_w_o, mlp_norm, mlp_w_up, mlp_w_down):
    b, s, d = x.shape
    assert d == D_MODEL and s % DSA_TK == 0 and s % min(MLA_TQ, s) == 0
    n_top = min(TOPK_MAX, s // 4)
    bf = jnp.bfloat16
    n_a = a_w_in.shape[0]
    n_b = b_w_dq.shape[0]
    cb, bres = _bias_tables(rel_bias)
    cos_f, sin_a, sin_b = _rope_tables_padded(s)
    row = lambda v: v.reshape(1, -1).astype(jnp.float32)

    o1 = A_HEADS * A_HEAD_DIM
    o2 = o1 + A_HEAD_DIM
    o3 = o2 + A_HEAD_DIM
    o4 = o3 + IDX_HEADS * IDX_DIM
    o5 = o4 + IDX_DIM

    h = x
    layer = 0
    for i in range(n_a):
        w_in = a_w_in[i]
        wq = _pad_heads(w_in[:, :o1], A_HEADS, A_HEAD_DIM).astype(bf)
        wr = jnp.concatenate([_pad_lanes(w_in[:, o1:o2]), _pad_lanes(w_in[:, o2:o3]), w_in[:, o3:o4],
                              _pad_lanes(w_in[:, o4:])], axis=1).astype(bf)
        gq = _pad_lanes(row(a_q_norm[i]) * (A_HEAD_DIM ** -0.5 * LOG2E))
        gk = _pad_lanes(row(a_k_norm[i]))
        q_pad, kv, qi, misc = _dsa_proj(h, row(a_attn_norm[i]), wq, wr, gq, gk, cb)
        ki = misc[..., :IDX_DIM].astype(bf)
        wi_t = jnp.swapaxes(misc[..., IDX_DIM:IDX_DIM + IDX_HEADS] * (IDX_HEADS ** -0.5), 1, 2)
        mask_t = _dsa_select(ki, qi, wi_t, n_top)
        attn = _dsa_attn(q_pad, kv, mask_t, bres)
        h2 = _out_proj_residual(h.reshape(b * s, d), attn.reshape(b * s, -1), a_w_o[i].astype(bf))
        h2 = _mlp(h2, row(mlp_norm[layer]), mlp_w_up[layer].astype(bf), mlp_w_down[layer].astype(bf))
        h = h2.reshape(b, s, d)
        layer += 1

    w_ukv3 = w_ukv.reshape(KV_LORA, B_HEADS, QK_NOPE + V_DIM)
    wk = _pad_lanes(w_ukv3[:, :, :QK_NOPE]).reshape(KV_LORA, B_HEADS * LANES).astype(bf)
    wv = _pad_lanes(w_ukv3[:, :, QK_NOPE:]).reshape(KV_LORA, B_HEADS * LANES).astype(bf)
    w_kr = jnp.pad(w_dkv[:, KV_LORA:], ((0, 0), (QK_NOPE, LANES - QK_DIM))).astype(bf)
    k_pad, v_aug = _mla_kv_prep(h, row(kv_norm), w_dkv[:, :KV_LORA].astype(bf), w_kr, row(kv_lora_norm),
                                wk, wv, _pad_lanes(row(k_norm)), cos_f, sin_a, sin_b)
    for j in range(n_b):
        wu = _pad_heads(b_w_uq[j], B_HEADS, QK_DIM).astype(bf)
        gq = _pad_lanes(row(b_q_norm[j]) * (QK_DIM ** -0.5 * LOG2E))
        q_pad = _mla_q_prep(h, row(b_attn_norm[j]), b_w_dq[j].astype(bf), row(b_q_lora_norm[j]), wu, gq,
                            cos_f, sin_a, sin_b)
        attn = _mla_attn(q_pad, k_pad, v_aug)
        h2 = _out_proj_residual(h.reshape(b * s, d), attn.reshape(b * s, -1), b_w_o[j].astype(bf))
        h2 = _mlp(h2, row(mlp_norm[layer]), mlp_w_up[layer].astype(bf), mlp_w_down[layer].astype(bf))
        h = h2.reshape(b, s, d)
        layer += 1
    return h
```

```python
import functools
import math

import numpy as np
import jax
import jax.numpy as jnp
from jax import lax
from jax.experimental import pallas as pl
from jax.experimental.pallas import tpu as pltpu

D_MODEL = 1024
CHUNK = 64
CHUNK_SHIFT = 6
EPS = 1e-6
NEG = -1e30
LOG2E = 1.4426950408889634

A_HEADS = 16
A_HEAD_DIM = 64
IDX_HEADS = 8
IDX_DIM = 64
TOPK_MAX = 256
N_BUCKETS = 32
MAX_DISTANCE = 128

B_HEADS = 16
QK_NOPE = 64
QK_ROPE = 32
QK_DIM = QK_NOPE + QK_ROPE
V_DIM = 64
KV_LORA = 256
Q_LORA = 384
ROPE_THETA = 10000.0
D_FF = 4 * D_MODEL

LANES = 128
INT_MIN = -(2 ** 31)
VMEM_LIMIT = 56 * 1024 * 1024

ROW_TILE = 512
DSA_TQ = 128
DSA_TK = 256
DSA_ROW_GROUP = 512
SEL_RB = 256
MLA_TQ = 512
MLA_ROW_GROUP = 256

_NT = (((1,), (1,)), ((), ()))


def _cparams(sem):
    return pltpu.CompilerParams(dimension_semantics=sem, vmem_limit_bytes=VMEM_LIMIT)


def _rms(x, g):
    return x * lax.rsqrt(jnp.mean(x * x, axis=-1, keepdims=True) + EPS) * g


def _rms_padded(x, g, true_dim):
    ms = jnp.sum(x * x, axis=-1, keepdims=True) * (1.0 / true_dim)
    return x * lax.rsqrt(ms + EPS) * g


def _dsa_proj_kernel(h_ref, g_ref, wq_ref, wr_ref, gq_ref, gk_ref, cb_ref, q_ref, kv_ref, qi_ref, misc_ref):
    xn = _rms(h_ref[0], g_ref[...]).astype(jnp.bfloat16)
    yq = jnp.dot(xn, wq_ref[...], preferred_element_type=jnp.float32)
    for h in range(A_HEADS):
        qh = yq[:, h * LANES:(h + 1) * LANES]
        qn = _rms_padded(qh, gq_ref[...], A_HEAD_DIM) + cb_ref[h:h + 1, :]
        q_ref[0, h] = qn.astype(jnp.bfloat16)
    yr = jnp.dot(xn, wr_ref[...], preferred_element_type=jnp.float32)
    lane = lax.broadcasted_iota(jnp.int32, (1, LANES), 1)
    k_ones = jnp.where((lane == A_HEAD_DIM) | (lane == A_HEAD_DIM + 1), 1.0, 0.0)
    v_ones = jnp.where(lane == A_HEAD_DIM, 1.0, 0.0)
    kn = _rms_padded(yr[:, :LANES], gk_ref[...], A_HEAD_DIM) + k_ones
    vv = yr[:, LANES:2 * LANES] + v_ones
    kv_ref[0, :, :LANES] = kn.astype(jnp.bfloat16)
    kv_ref[0, :, LANES:] = vv.astype(jnp.bfloat16)
    qi = yr[:, 2 * LANES:2 * LANES + IDX_HEADS * IDX_DIM] * (IDX_DIM ** -0.5)
    for h in range(IDX_HEADS):
        qi_ref[0, h] = qi[:, h * IDX_DIM:(h + 1) * IDX_DIM].astype(jnp.bfloat16)
    misc_ref[0] = yr[:, 2 * LANES + IDX_HEADS * IDX_DIM:]


def _dsa_proj(h, g, wq, wr, gq, gk, cb):
    b, s, d = h.shape
    tm = min(ROW_TILE, s)
    nr = wr.shape[1]
    const = lambda shape: pl.BlockSpec(shape, lambda i, j: (0,) * len(shape))
    return pl.pallas_call(
        _dsa_proj_kernel,
        grid=(b, s // tm),
        in_specs=[
            pl.BlockSpec((1, tm, d), lambda i, j: (i, j, 0)),
            const((1, d)), const(wq.shape), const(wr.shape), const((1, LANES)), const((1, LANES)),
            const((A_HEADS, LANES)),
        ],
        out_specs=[
            pl.BlockSpec((1, A_HEADS, tm, LANES), lambda i, j: (i, 0, j, 0)),
            pl.BlockSpec((1, tm, 2 * LANES), lambda i, j: (i, j, 0)),
            pl.BlockSpec((1, IDX_HEADS, tm, IDX_DIM), lambda i, j: (i, 0, j, 0)),
            pl.BlockSpec((1, tm, LANES), lambda i, j: (i, j, 0)),
        ],
        out_shape=[
            jax.ShapeDtypeStruct((b, A_HEADS, s, LANES), jnp.bfloat16),
            jax.ShapeDtypeStruct((b, s, 2 * LANES), jnp.bfloat16),
            jax.ShapeDtypeStruct((b, IDX_HEADS, s, IDX_DIM), jnp.bfloat16),
            jax.ShapeDtypeStruct((b, s, LANES), jnp.float32),
        ],
        compiler_params=_cparams(("parallel", "parallel")),
    )(h, g, wq, wr, gq, gk, cb)


def _dsa_select_kernel(ki_ref, qi_ref, wi_ref, mask_ref, keys_scr, *, n_top, seq):
    rb = SEL_RB
    t0 = pl.program_id(1) * DSA_TQ
    nblk = (t0 + DSA_TQ + rb - 1) // rb
    qi2d = qi_ref[0].reshape(IDX_HEADS * DSA_TQ, IDX_DIM)
    w = wi_ref[0]
    t_chunk = (t0 + lax.broadcasted_iota(jnp.int32, (1, DSA_TQ), 1)) >> CHUNK_SHIFT
    row_iota = lax.broadcasted_iota(jnp.int32, (rb, DSA_TQ), 0)

    def admissible(r0):
        return ((r0 + row_iota) >> CHUNK_SHIFT) <= t_chunk

    def score_blk(blk, carry):
        r0 = pl.multiple_of(blk * rb, rb)
        d = lax.dot_general(ki_ref[0, pl.ds(r0, rb), :], qi2d, _NT, preferred_element_type=jnp.float32)
        acc = jnp.zeros((rb, DSA_TQ), jnp.float32)
        for h in range(IDX_HEADS):
            acc = acc + w[h:h + 1, :] * jnp.maximum(d[:, h * DSA_TQ:(h + 1) * DSA_TQ], 0.0)
        bits = lax.bitcast_convert_type(acc, jnp.int32)
        key = bits ^ ((bits >> 31) & 0x7FFFFFFF)
        keys_scr[pl.ds(r0, rb), :] = jnp.where(admissible(r0), key, INT_MIN)
        return carry

    lax.fori_loop(0, nblk, score_blk, 0)

    def count(pred_fn):
        def blk_body(blk, cnt):
            r0 = pl.multiple_of(blk * rb, rb)
            hit = jnp.where(pred_fn(keys_scr[pl.ds(r0, rb), :]), 1, 0)
            return cnt + jnp.sum(hit.reshape(rb // 8, 8, DSA_TQ), axis=0)
        cnt8 = lax.fori_loop(0, nblk, blk_body, jnp.zeros((8, DSA_TQ), jnp.int32))
        return jnp.sum(cnt8, axis=0, keepdims=True)

    def bit_pass(i, thr):
        trial = thr + lax.shift_left(jnp.int32(1), 31 - i)
        cnt = count(lambda k: k >= trial)
        return jnp.where(cnt >= n_top, trial, thr)

    thr = lax.fori_loop(0, 32, bit_pass, jnp.full((1, DSA_TQ), INT_MIN, jnp.int32))
    quota = (n_top - count(lambda k: k > thr)).astype(jnp.float32)

    tri = (lax.broadcasted_iota(jnp.int32, (rb, rb), 1)
           < lax.broadcasted_iota(jnp.int32, (rb, rb), 0))
    tri = jnp.where(tri, 1.0, 0.0).astype(jnp.bfloat16)

    def emit_blk(blk, seen):
        r0 = pl.multiple_of(blk * rb, rb)
        key = keys_scr[pl.ds(r0, rb), :]
        eq = key == thr
        eqf = jnp.where(eq, 1.0, 0.0)
        before = jnp.dot(tri, eqf.astype(jnp.bfloat16), preferred_element_type=jnp.float32) + seen
        sel = (key > thr) | (eq & (before < quota))
        sel = sel & admissible(r0)
        mask_ref[0, pl.ds(r0, rb), :] = jnp.where(sel, 0.0, NEG).astype(jnp.bfloat16)
        return seen + jnp.sum(eqf, axis=0, keepdims=True)

    lax.fori_loop(0, nblk, emit_blk, jnp.zeros((1, DSA_TQ), jnp.float32))

    def fill_blk(blk, carry):
        r0 = pl.multiple_of(blk * rb, rb)
        mask_ref[0, pl.ds(r0, rb), :] = jnp.full((rb, DSA_TQ), NEG, jnp.bfloat16)
        return carry

    lax.fori_loop(nblk, seq // rb, fill_blk, 0)


def _dsa_select(ki, qi, wi_t, n_top):
    b, s, _ = ki.shape
    return pl.pallas_call(
        functools.partial(_dsa_select_kernel, n_top=n_top, seq=s),
        grid=(b, s // DSA_TQ),
        in_specs=[
            pl.BlockSpec((1, s, IDX_DIM), lambda i, j: (i, 0, 0)),
            pl.BlockSpec((1, IDX_HEADS, DSA_TQ, IDX_DIM), lambda i, j: (i, 0, j, 0)),
            pl.BlockSpec((1, IDX_HEADS, DSA_TQ), lambda i, j: (i, 0, j)),
        ],
        out_specs=pl.BlockSpec((1, s, DSA_TQ), lambda i, j: (i, 0, j)),
        out_shape=jax.ShapeDtypeStruct((b, s, s), jnp.bfloat16),
        scratch_shapes=[pltpu.VMEM((s, DSA_TQ), jnp.int32)],
        compiler_params=_cparams(("parallel", "parallel")),
    )(ki, qi, wi_t)


def _dsa_attn_kernel(q_ref, kv_ref, mask_ref, bres_ref, o_ref, lhs_scr, m_scr, acc_scr):
    rows = A_HEADS * DSA_TQ
    qidx = pl.program_id(1)
    nk = (qidx * DSA_TQ + DSA_TQ + DSA_TK - 1) // DSA_TK
    odd = qidx % 2

    lhs_scr[:, :LANES] = q_ref[0].reshape(rows, LANES)
    r = lax.broadcasted_iota(jnp.int32, (rows, DSA_TQ), 0)
    c = lax.broadcasted_iota(jnp.int32, (rows, DSA_TQ), 1)
    lhs_scr[:, LANES:] = jnp.where((r & (DSA_TQ - 1)) == c, 1.0, 0.0).astype(jnp.bfloat16)
    m_scr[...] = jnp.full(m_scr.shape, -jnp.inf, jnp.float32)
    acc_scr[...] = jnp.zeros(acc_scr.shape, jnp.float32)

    def tile_body(j, carry):
        k0 = pl.multiple_of(j * DSA_TK, DSA_TK)
        rhs = jnp.concatenate([kv_ref[0, pl.ds(k0, DSA_TK), :LANES], mask_ref[0, pl.ds(k0, DSA_TK), :]], axis=1)
        v_tile = kv_ref[0, pl.ds(k0, DSA_TK), LANES:]
        cfg = jnp.where(j == nk - 1, odd, jnp.where((j == nk - 2) & (odd == 0), 2, 3))
        for g in range(rows // DSA_ROW_GROUP):
            g0 = g * DSA_ROW_GROUP
            s_g = lax.dot_general(lhs_scr[g0:g0 + DSA_ROW_GROUP, :], rhs, _NT,
                                  preferred_element_type=jnp.float32)
            p_parts, alphas = [], []
            for hh in range(DSA_ROW_GROUP // DSA_TQ):
                r0 = g0 + hh * DSA_TQ
                s = s_g[hh * DSA_TQ:(hh + 1) * DSA_TQ] + bres_ref[cfg, r0:r0 + DSA_TQ, :]
                m_old = m_scr[r0:r0 + DSA_TQ, :]
                m_new = jnp.maximum(m_old, jnp.max(s, axis=1, keepdims=True))
                alphas.append(jnp.exp2(m_old - m_new))
                p_parts.append(jnp.exp2((s - jnp.concatenate([m_new, m_new], axis=1)).astype(jnp.bfloat16)))
                m_scr[r0:r0 + DSA_TQ, :] = m_new
            pv = jnp.dot(jnp.concatenate(p_parts, axis=0), v_tile, preferred_element_type=jnp.float32)
            acc_scr[g0:g0 + DSA_ROW_GROUP, :] = (acc_scr[g0:g0 + DSA_ROW_GROUP, :]
                                                 * jnp.concatenate(alphas, axis=0) + pv)
        return carry

    lax.fori_loop(0, nk, tile_body, 0)

    for h in range(A_HEADS):
        a = acc_scr[h * DSA_TQ:(h + 1) * DSA_TQ, :]
        o = a[:, :A_HEAD_DIM] / a[:, A_HEAD_DIM:A_HEAD_DIM + 1]
        o_ref[0, :, h * A_HEAD_DIM:(h + 1) * A_HEAD_DIM] = o.astype(jnp.bfloat16)


def _dsa_attn(q_pad, kv, mask_t, bres):
    b, _, s, _ = q_pad.shape
    rows = A_HEADS * DSA_TQ
    return pl.pallas_call(
        _dsa_attn_kernel,
        grid=(b, s // DSA_TQ),
        in_specs=[
            pl.BlockSpec((1, A_HEADS, DSA_TQ, LANES), lambda i, j: (i, 0, j, 0)),
            pl.BlockSpec((1, s, 2 * LANES), lambda i, j: (i, 0, 0)),
            pl.BlockSpec((1, s, DSA_TQ), lambda i, j: (i, 0, j)),
            pl.BlockSpec(bres.shape, lambda i, j: (0, 0, 0)),
        ],
        out_specs=pl.BlockSpec((1, DSA_TQ, A_HEADS * A_HEAD_DIM), lambda i, j: (i, j, 0)),
        out_shape=jax.ShapeDtypeStruct((b, s, A_HEADS * A_HEAD_DIM), jnp.bfloat16),
        scratch_shapes=[
            pltpu.VMEM((rows, 2 * LANES), jnp.bfloat16),
            pltpu.VMEM((rows, LANES), jnp.float32),
            pltpu.VMEM((rows, LANES), jnp.float32),
        ],
        compiler_params=_cparams(("parallel", "parallel")),
    )(q_pad, kv, mask_t, bres)


def _out_proj_kernel(h_ref, a_ref, w_ref, o_ref):
    o_ref[...] = h_ref[...] + jnp.dot(a_ref[...], w_ref[...], preferred_element_type=jnp.float32)


def _out_proj_residual(h2d, a2d, w):
    n, d = h2d.shape
    tm = min(ROW_TILE, n)
    return pl.pallas_call(
        _out_proj_kernel,
        grid=(n // tm,),
        in_specs=[
            pl.BlockSpec((tm, d), lambda i: (i, 0)),
            pl.BlockSpec((tm, a2d.shape[1]), lambda i: (i, 0)),
            pl.BlockSpec(w.shape, lambda i: (0, 0)),
        ],
        out_specs=pl.BlockSpec((tm, d), lambda i: (i, 0)),
        out_shape=jax.ShapeDtypeStruct((n, d), jnp.float32),
        compiler_params=_cparams(("parallel",)),
    )(h2d, a2d, w)


def _mlp_kernel(h_ref, g_ref, wu_ref, wd_ref, o_ref, *, tf):
    x = h_ref[...]
    xn = _rms(x, g_ref[...]).astype(jnp.bfloat16)
    o_ref[...] = x
    for f in range(0, D_FF, tf):
        u = jnp.dot(xn, wu_ref[:, f:f + tf], preferred_element_type=jnp.float32)
        a = jnp.square(jnp.maximum(u, 0.0)).astype(jnp.bfloat16)
        o_ref[...] += jnp.dot(a, wd_ref[f:f + tf, :], preferred_element_type=jnp.float32)


def _mlp(h2d, g, wu, wd):
    n, d = h2d.shape
    tm = min(ROW_TILE, n)
    resident = lambda shape: pl.BlockSpec(shape, lambda i: (0, 0), pipeline_mode=pl.Buffered(1))
    return pl.pallas_call(
        functools.partial(_mlp_kernel, tf=1024),
        grid=(n // tm,),
        in_specs=[
            pl.BlockSpec((tm, d), lambda i: (i, 0)),
            pl.BlockSpec((1, d), lambda i: (0, 0)),
            resident(wu.shape), resident(wd.shape),
        ],
        out_specs=pl.BlockSpec((tm, d), lambda i: (i, 0)),
        out_shape=jax.ShapeDtypeStruct((n, d), jnp.float32),
        compiler_params=_cparams(("parallel",)),
    )(h2d, g, wu, wd)


def _rope_padded(x, cos_f, sin_a, sin_b):
    return x * cos_f + pltpu.roll(x, LANES - QK_ROPE // 2, 1) * sin_a + pltpu.roll(x, QK_ROPE // 2, 1) * sin_b


def _mla_kv_kernel(h_ref, g_ref, wc_ref, wr_ref, gl_ref, wk_ref, wv_ref, gk_ref, cos_ref, sa_ref, sb_ref,
                   k_ref, v_ref):
    xn = _rms(h_ref[0], g_ref[...]).astype(jnp.bfloat16)
    c = jnp.dot(xn, wc_ref[...], preferred_element_type=jnp.float32)
    c = _rms(c, gl_ref[...]).astype(jnp.bfloat16)
    k_rope = jnp.dot(xn, wr_ref[...], preferred_element_type=jnp.float32)
    kn = jnp.dot(c, wk_ref[...], preferred_element_type=jnp.float32)
    vv = jnp.dot(c, wv_ref[...], preferred_element_type=jnp.float32)
    lane = lax.broadcasted_iota(jnp.int32, (1, LANES), 1)
    v_ones = jnp.where(lane == V_DIM, 1.0, 0.0)
    cos_f, sin_a, sin_b = cos_ref[...], sa_ref[...], sb_ref[...]
    for h in range(B_HEADS):
        kh = kn[:, h * LANES:(h + 1) * LANES] + k_rope
        kh = _rope_padded(_rms_padded(kh, gk_ref[...], QK_DIM), cos_f, sin_a, sin_b)
        k_ref[0, h] = kh.astype(jnp.bfloat16)
        v_ref[0, h] = (vv[:, h * LANES:(h + 1) * LANES] + v_ones).astype(jnp.bfloat16)


def _mla_kv_prep(h, g, wc, wr, gl, wk, wv, gk, cos_f, sin_a, sin_b):
    b, s, d = h.shape
    tm = min(ROW_TILE, s)
    const = lambda shape: pl.BlockSpec(shape, lambda i, j: (0,) * len(shape))
    pos = pl.BlockSpec((tm, LANES), lambda i, j: (j, 0))
    head_out = pl.BlockSpec((1, B_HEADS, tm, LANES), lambda i, j: (i, 0, j, 0))
    return pl.pallas_call(
        _mla_kv_kernel,
        grid=(b, s // tm),
        in_specs=[
            pl.BlockSpec((1, tm, d), lambda i, j: (i, j, 0)),
            const((1, d)), const(wc.shape), const(wr.shape), const((1, KV_LORA)), const(wk.shape),
            const(wv.shape), const((1, LANES)), pos, pos, pos,
        ],
        out_specs=[head_out, head_out],
        out_shape=[jax.ShapeDtypeStruct((b, B_HEADS, s, LANES), jnp.bfloat16)] * 2,
        compiler_params=_cparams(("parallel", "parallel")),
    )(h, g, wc, wr, gl, wk, wv, gk, cos_f, sin_a, sin_b)


def _mla_q_kernel(h_ref, g_ref, wd_ref, gl_ref, wu_ref, gq_ref, cos_ref, sa_ref, sb_ref, q_ref):
    xn = _rms(h_ref[0], g_ref[...]).astype(jnp.bfloat16)
    c = jnp.dot(xn, wd_ref[...], preferred_element_type=jnp.float32)
    c = _rms(c, gl_ref[...]).astype(jnp.bfloat16)
    yq = jnp.dot(c, wu_ref[...], preferred_element_type=jnp.float32)
    cos_f, sin_a, sin_b = cos_ref[...], sa_ref[...], sb_ref[...]
    for h in range(B_HEADS):
        qh = _rms_padded(yq[:, h * LANES:(h + 1) * LANES], gq_ref[...], QK_DIM)
        q_ref[0, h] = _rope_padded(qh, cos_f, sin_a, sin_b).astype(jnp.bfloat16)


def _mla_q_prep(h, g, wd, gl, wu, gq, cos_f, sin_a, sin_b):
    b, s, d = h.shape
    tm = min(ROW_TILE, s)
    const = lambda shape: pl.BlockSpec(shape, lambda i, j: (0,) * len(shape))
    pos = pl.BlockSpec((tm, LANES), lambda i, j: (j, 0))
    return pl.pallas_call(
        _mla_q_kernel,
        grid=(b, s // tm),
        in_specs=[
            pl.BlockSpec((1, tm, d), lambda i, j: (i, j, 0)),
            const((1, d)), const(wd.shape), const((1, Q_LORA)), const(wu.shape), const((1, LANES)),
            pos, pos, pos,
        ],
        out_specs=pl.BlockSpec((1, B_HEADS, tm, LANES), lambda i, j: (i, 0, j, 0)),
        out_shape=jax.ShapeDtypeStruct((b, B_HEADS, s, LANES), jnp.bfloat16),
        compiler_params=_cparams(("parallel", "parallel")),
    )(h, g, wd, gl, wu, gq, cos_f, sin_a, sin_b)


def _mla_attn_kernel(q_ref, k_ref, v_ref, o_ref, m_scr, acc_scr, *, tq):
    rg = min(MLA_ROW_GROUP, tq)
    diag = pl.program_id(2)
    m_scr[...] = jnp.full(m_scr.shape, -jnp.inf, jnp.float32)
    acc_scr[...] = jnp.zeros(acc_scr.shape, jnp.float32)

    def tile(j, masked):
        k0 = pl.multiple_of(j * tq, tq)
        for hh in range(2):
            k_tile = k_ref[0, hh, pl.ds(k0, tq), :]
            v_tile = v_ref[0, hh, pl.ds(k0, tq), :]
            for g0 in range(0, tq, rg):
                s = lax.dot_general(q_ref[0, hh, g0:g0 + rg, :], k_tile, _NT, preferred_element_type=jnp.float32)
                if masked:
                    q_chunk = (g0 + lax.broadcasted_iota(jnp.int32, (rg, tq), 0)) >> CHUNK_SHIFT
                    k_chunk = lax.broadcasted_iota(jnp.int32, (rg, tq), 1) >> CHUNK_SHIFT
                    s = jnp.where(k_chunk <= q_chunk, s, NEG)
                m_old = m_scr[hh, g0:g0 + rg, :]
                m_new = jnp.maximum(m_old, jnp.max(s, axis=1, keepdims=True))
                alpha = jnp.exp2(m_old - m_new)
                p = jnp.exp2((s - jnp.concatenate([m_new] * (tq // LANES), axis=1)).astype(jnp.bfloat16))
                pv = jnp.dot(p, v_tile, preferred_element_type=jnp.float32)
                acc_scr[hh, g0:g0 + rg, :] = acc_scr[hh, g0:g0 + rg, :] * alpha + pv
                m_scr[hh, g0:g0 + rg, :] = m_new

    def full_tile(j, carry):
        tile(j, False)
        return carry

    lax.fori_loop(0, diag, full_tile, 0)
    tile(diag, True)
    outs = []
    for hh in range(2):
        a = acc_scr[hh]
        outs.append(a[:, :V_DIM] / a[:, V_DIM:V_DIM + 1])
    o_ref[0] = jnp.concatenate(outs, axis=1).astype(jnp.bfloat16)


def _mla_attn(q_pad, k_pad, v_aug):
    b, nh, s, _ = q_pad.shape
    tq = min(MLA_TQ, s)
    return pl.pallas_call(
        functools.partial(_mla_attn_kernel, tq=tq),
        grid=(b, nh // 2, s // tq),
        in_specs=[
            pl.BlockSpec((1, 2, tq, LANES), lambda i, p, j: (i, p, j, 0)),
            pl.BlockSpec((1, 2, s, LANES), lambda i, p, j: (i, p, 0, 0)),
            pl.BlockSpec((1, 2, s, LANES), lambda i, p, j: (i, p, 0, 0)),
        ],
        out_specs=pl.BlockSpec((1, tq, 2 * V_DIM), lambda i, p, j: (i, j, p)),
        out_shape=jax.ShapeDtypeStruct((b, s, nh * V_DIM), jnp.bfloat16),
        scratch_shapes=[pltpu.VMEM((2, tq, LANES), jnp.float32), pltpu.VMEM((2, tq, LANES), jnp.float32)],
        compiler_params=_cparams(("parallel", "parallel", "parallel")),
    )(q_pad, k_pad, v_aug)


def _t5_bucket(rel):
    nb = N_BUCKETS // 2
    max_exact = nb // 2
    ret = jnp.where(rel > 0, nb, 0)
    n = jnp.abs(rel)
    nf = jnp.maximum(n, 1).astype(jnp.float32)
    large = max_exact + (jnp.log(nf / max_exact) / math.log(MAX_DISTANCE / max_exact)
                         * (nb - max_exact)).astype(jnp.int32)
    large = jnp.minimum(large, nb - 1)
    return ret + jnp.where(n < max_exact, n, large)


def _pad_lanes(a, width=LANES):
    return jnp.pad(a, [(0, 0)] * (a.ndim - 1) + [(0, width - a.shape[-1])])


def _pad_heads(w, n_heads, head_dim):
    k = w.shape[0]
    return _pad_lanes(w.reshape(k, n_heads, head_dim)).reshape(k, n_heads * LANES)


def _bias_tables(rel_bias):
    far = rel_bias[N_BUCKETS // 2 - 1].astype(jnp.float32) * LOG2E
    c_hi = far.astype(jnp.bfloat16)
    c_lo = (far - c_hi.astype(jnp.float32)).astype(jnp.bfloat16)
    far_eff = c_hi.astype(jnp.float32) + c_lo.astype(jnp.float32)
    cb = jnp.zeros((A_HEADS, LANES), jnp.float32)
    cb = cb.at[:, A_HEAD_DIM].set(c_hi.astype(jnp.float32)).at[:, A_HEAD_DIM + 1].set(c_lo.astype(jnp.float32))
    r = jnp.arange(DSA_TQ, dtype=jnp.int32)[:, None]
    c = jnp.arange(DSA_TK, dtype=jnp.int32)[None, :]
    tiles = []
    for off in (0, DSA_TQ, DSA_TK):
        rel = c - r - off
        bias = rel_bias[_t5_bucket(rel)].astype(jnp.float32) * LOG2E
        tiles.append(jnp.transpose(bias, (2, 0, 1)) - far_eff[:, None, None])
    tiles.append(jnp.zeros_like(tiles[0]))
    bres = jnp.stack(tiles).reshape(4, A_HEADS * DSA_TQ, DSA_TK)
    return cb, bres


def _rope_tables_padded(seq_len):
    pos = jnp.arange(seq_len, dtype=jnp.float32)
    inv_freq = 1.0 / (ROPE_THETA ** (jnp.arange(0, QK_ROPE, 2, dtype=jnp.float32) / QK_ROPE))
    ang = pos[:, None] * inv_freq[None, :]
    cos, sin = jnp.cos(ang), jnp.sin(ang)
    half = QK_ROPE // 2
    ones = jnp.ones((seq_len, QK_NOPE), jnp.float32)
    zeros = jnp.zeros((seq_len, QK_NOPE), jnp.float32)
    zh = jnp.zeros((seq_len, half), jnp.float32)
    cos_f = _pad_lanes(jnp.concatenate([ones, cos, cos], axis=1))
    sin_a = _pad_lanes(jnp.concatenate([zeros, -sin, zh], axis=1))
    sin_b = _pad_lanes(jnp.concatenate([zeros, zh, sin], axis=1))
    return cos_f, sin_a, sin_b


def kernel(x, rel_bias, a_attn_norm, a_w_in, a_q_norm, a_k_norm, a_w_o, kv_norm, w_dkv, kv_lora_norm, w_ukv, k_norm, b_attn_norm, b_w_dq, b_q_lora_norm, b_w_uq, b_q_norm, b_w_o, mlp_norm, mlp_w_up, mlp_w_down):
    b, s, d = x.shape
    assert d == D_MODEL and s % DSA_TK == 0 and s % min(MLA_TQ, s) == 0
    n_top = min(TOPK_MAX, s // 4)
    bf = jnp.bfloat16
    n_a = a_w_in.shape[0]
    n_b = b_w_dq.shape[0]
    cb, bres = _bias_tables(rel_bias)
    cos_f, sin_a, sin_b = _rope_tables_padded(s)
    row = lambda v: v.reshape(1, -1).astype(jnp.float32)

    o1 = A_HEADS * A_HEAD_DIM
    o2 = o1 + A_HEAD_DIM
    o3 = o2 + A_HEAD_DIM
    o4 = o3 + IDX_HEADS * IDX_DIM
    o5 = o4 + IDX_DIM

    h = x
    layer = 0
    for i in range(n_a):
        w_in = a_w_in[i]
        wq = _pad_heads(w_in[:, :o1], A_HEADS, A_HEAD_DIM).astype(bf)
        wr = jnp.concatenate([_pad_lanes(w_in[:, o1:o2]), _pad_lanes(w_in[:, o2:o3]), w_in[:, o3:o4],
                              _pad_lanes(w_in[:, o4:])], axis=1).astype(bf)
        gq = _pad_lanes(row(a_q_norm[i]) * (A_HEAD_DIM ** -0.5 * LOG2E))
        gk = _pad_lanes(row(a_k_norm[i]))
        q_pad, kv, qi, misc = _dsa_proj(h, row(a_attn_norm[i]), wq, wr, gq, gk, cb)
        ki = misc[..., :IDX_DIM].astype(bf)
        wi_t = jnp.swapaxes(misc[..., IDX_DIM:IDX_DIM + IDX_HEADS] * (IDX_HEADS ** -0.5), 1, 2)
        mask_t = _dsa_select(ki, qi, wi_t, n_top)
        attn = _dsa_attn(q_pad, kv, mask_t, bres)
        h2 = _out_proj_residual(h.reshape(b * s, d), attn.reshape(b * s, -1), a_w_o[i].astype(bf))
        h2 = _mlp(h2, row(mlp_norm[layer]), mlp_w_up[layer].astype(bf), mlp_w_down[layer].astype(bf))
        h = h2.reshape(b, s, d)
        layer += 1

    w_ukv3 = w_ukv.reshape(KV_LORA, B_HEADS, QK_NOPE + V_DIM)
    wk = _pad_lanes(w_ukv3[:, :, :QK_NOPE]).reshape(KV_LORA, B_HEADS * LANES).astype(bf)
    wv = _pad_lanes(w_ukv3[:, :, QK_NOPE:]).reshape(KV_LORA, B_HEADS * LANES).astype(bf)
    w_kr = jnp.pad(w_dkv[:, KV_LORA:], ((0, 0), (QK_NOPE, LANES - QK_DIM))).astype(bf)
    k_pad, v_aug = _mla_kv_prep(h, row(kv_norm), w_dkv[:, :KV_LORA].astype(bf), w_kr, row(kv_lora_norm),
                                wk, wv, _pad_lanes(row(k_norm)), cos_f, sin_a, sin_b)
    for j in range(n_b):
        wu = _pad_heads(b_w_uq[j], B_HEADS, QK_DIM).astype(bf)
        gq = _pad_lanes(row(b_q_norm[j]) * (QK_DIM ** -0.5 * LOG2E))
        q_pad = _mla_q_prep(h, row(b_attn_norm[j]), b_w_dq[j].astype(bf), row(b_q_lora_norm[j]), wu, gq,
                            cos_f, sin_a, sin_b)
        attn = _mla_attn(q_pad, k_pad, v_aug)
        h2 = _out_proj_residual(h.reshape(b * s, d), attn.reshape(b * s, -1), b_w_o[j].astype(bf))
        h2 = _mlp(h2, row(mlp_norm[layer]), mlp_w_up[layer].astype(bf), mlp_w_down[layer].astype(bf))
        h = h2.reshape(b, s, d)
        layer += 1
    return h
```

```python
import functools
import math

import numpy as np
import jax
import jax.numpy as jnp
from jax import lax
from jax.experimental import pallas as pl
from jax.experimental.pallas import tpu as pltpu

D_MODEL = 1024
CHUNK = 64
CHUNK_SHIFT = 6
EPS = 1e-6
NEG = -1e30
LOG2E = 1.4426950408889634

A_HEADS = 16
A_HEAD_DIM = 64
IDX_HEADS = 8
IDX_DIM = 64
TOPK_MAX = 256
N_BUCKETS = 32
MAX_DISTANCE = 128

B_HEADS = 16
QK_NOPE = 64
QK_ROPE = 32
QK_DIM = QK_NOPE + QK_ROPE
V_DIM = 64
KV_LORA = 256
Q_LORA = 384
ROPE_THETA = 10000.0
D_FF = 4 * D_MODEL

LANES = 128
INT_MIN = -(2 ** 31)
VMEM_LIMIT = 56 * 1024 * 1024

ROW_TILE = 512
DSA_TQ = 128
DSA_TK = 256
SEL_RB = 256
MLA_TQ = 512
MLA_ROW_GROUP = 128

_NT = (((1,), (1,)), ((), ()))


def _cparams(sem):
    return pltpu.CompilerParams(dimension_semantics=sem, vmem_limit_bytes=VMEM_LIMIT)


def _rms(x, g):
    return x * lax.rsqrt(jnp.mean(x * x, axis=-1, keepdims=True) + EPS) * g


def _rms_padded(x, g, true_dim):
    ms = jnp.sum(x * x, axis=-1, keepdims=True) * (1.0 / true_dim)
    return x * lax.rsqrt(ms + EPS) * g


def _dsa_proj_kernel(h_ref, g_ref, wq_ref, wr_ref, gq_ref, gk_ref, cb_ref, q_ref, kv_ref, qi_ref, misc_ref):
    xn = _rms(h_ref[0], g_ref[...]).astype(jnp.bfloat16)
    yq = jnp.dot(xn, wq_ref[...], preferred_element_type=jnp.float32)
    for h in range(A_HEADS):
        qh = yq[:, h * LANES:(h + 1) * LANES]
        qn = _rms_padded(qh, gq_ref[...], A_HEAD_DIM) + cb_ref[h:h + 1, :]
        q_ref[0, h] = qn.astype(jnp.bfloat16)
    yr = jnp.dot(xn, wr_ref[...], preferred_element_type=jnp.float32)
    lane = lax.broadcasted_iota(jnp.int32, (1, LANES), 1)
    k_ones = jnp.where((lane == A_HEAD_DIM) | (lane == A_HEAD_DIM + 1), 1.0, 0.0)
    v_ones = jnp.where(lane == A_HEAD_DIM, 1.0, 0.0)
    kn = _rms_padded(yr[:, :LANES], gk_ref[...], A_HEAD_DIM) + k_ones
    vv = yr[:, LANES:2 * LANES] + v_ones
    kv_ref[0, :, :LANES] = kn.astype(jnp.bfloat16)
    kv_ref[0, :, LANES:] = vv.astype(jnp.bfloat16)
    qi = yr[:, 2 * LANES:2 * LANES + IDX_HEADS * IDX_DIM] * (IDX_DIM ** -0.5)
    for h in range(IDX_HEADS):
        qi_ref[0, h] = qi[:, h * IDX_DIM:(h + 1) * IDX_DIM].astype(jnp.bfloat16)
    misc_ref[0] = yr[:, 2 * LANES + IDX_HEADS * IDX_DIM:]


def _dsa_proj(h, g, wq, wr, gq, gk, cb):
    b, s, d = h.shape
    tm = min(ROW_TILE, s)
    const = lambda shape: pl.BlockSpec(shape, lambda i, j: (0,) * len(shape))
    return pl.pallas_call(
        _dsa_proj_kernel,
        grid=(b, s // tm),
        in_specs=[
            pl.BlockSpec((1, tm, d), lambda i, j: (i, j, 0)),
            const((1, d)), const(wq.shape), const(wr.shape), const((1, LANES)), const((1, LANES)),
            const((A_HEADS, LANES)),
        ],
        out_specs=[
            pl.BlockSpec((1, A_HEADS, tm, LANES), lambda i, j: (i, 0, j, 0)),
            pl.BlockSpec((1, tm, 2 * LANES), lambda i, j: (i, j, 0)),
            pl.BlockSpec((1, IDX_HEADS, tm, IDX_DIM), lambda i, j: (i, 0, j, 0)),
            pl.BlockSpec((1, tm, LANES), lambda i, j: (i, j, 0)),
        ],
        out_shape=[
            jax.ShapeDtypeStruct((b, A_HEADS, s, LANES), jnp.bfloat16),
            jax.ShapeDtypeStruct((b, s, 2 * LANES), jnp.bfloat16),
            jax.ShapeDtypeStruct((b, IDX_HEADS, s, IDX_DIM), jnp.bfloat16),
            jax.ShapeDtypeStruct((b, s, LANES), jnp.float32),
        ],
        compiler_params=_cparams(("parallel", "parallel")),
    )(h, g, wq, wr, gq, gk, cb)


def _dsa_select_kernel(ki_ref, qi_ref, wi_ref, mask_ref, keys_scr, *, n_top, seq):
    rb = SEL_RB
    t0 = pl.program_id(1) * DSA_TQ
    nblk = (t0 + DSA_TQ + rb - 1) // rb
    qi2d = qi_ref[0].reshape(IDX_HEADS * DSA_TQ, IDX_DIM)
    w = wi_ref[0]
    t_chunk = (t0 + lax.broadcasted_iota(jnp.int32, (1, DSA_TQ), 1)) >> CHUNK_SHIFT
    row_iota = lax.broadcasted_iota(jnp.int32, (rb, DSA_TQ), 0)

    def admissible(r0):
        return ((r0 + row_iota) >> CHUNK_SHIFT) <= t_chunk

    def score_blk(blk, carry):
        r0 = pl.multiple_of(blk * rb, rb)
        d = lax.dot_general(ki_ref[0, pl.ds(r0, rb), :], qi2d, _NT, preferred_element_type=jnp.float32)
        acc = jnp.zeros((rb, DSA_TQ), jnp.float32)
        for h in range(IDX_HEADS):
            acc = acc + w[h:h + 1, :] * jnp.maximum(d[:, h * DSA_TQ:(h + 1) * DSA_TQ], 0.0)
        bits = lax.bitcast_convert_type(acc, jnp.int32)
        key = bits ^ ((bits >> 31) & 0x7FFFFFFF)
        keys_scr[pl.ds(r0, rb), :] = jnp.where(admissible(r0), key, INT_MIN)
        return carry

    lax.fori_loop(0, nblk, score_blk, 0)

    def count(pred_fn):
        def blk_body(blk, cnt):
            r0 = pl.multiple_of(blk * rb, rb)
            hit = jnp.where(pred_fn(keys_scr[pl.ds(r0, rb), :]), 1, 0)
            return cnt + jnp.sum(hit.reshape(rb // 8, 8, DSA_TQ), axis=0)
        cnt8 = lax.fori_loop(0, nblk, blk_body, jnp.zeros((8, DSA_TQ), jnp.int32))
        return jnp.sum(cnt8, axis=0, keepdims=True)

    def bit_pass(i, thr):
        trial = thr + lax.shift_left(jnp.int32(1), 31 - i)
        cnt = count(lambda k: k >= trial)
        return jnp.where(cnt >= n_top, trial, thr)

    thr = lax.fori_loop(0, 32, bit_pass, jnp.full((1, DSA_TQ), INT_MIN, jnp.int32))
    quota = (n_top - count(lambda k: k > thr)).astype(jnp.float32)

    tri = (lax.broadcasted_iota(jnp.int32, (rb, rb), 1)
           < lax.broadcasted_iota(jnp.int32, (rb, rb), 0))
    tri = jnp.where(tri, 1.0, 0.0).astype(jnp.bfloat16)

    def emit_blk(blk, seen):
        r0 = pl.multiple_of(blk * rb, rb)
        key = keys_scr[pl.ds(r0, rb), :]
        eq = key == thr
        eqf = jnp.where(eq, 1.0, 0.0)
        before = jnp.dot(tri, eqf.astype(jnp.bfloat16), preferred_element_type=jnp.float32) + seen
        sel = (key > thr) | (eq & (before < quota))
        sel = sel & admissible(r0)
        mask_ref[0, pl.ds(r0, rb), :] = jnp.where(sel, 0.0, NEG).astype(jnp.bfloat16)
        return seen + jnp.sum(eqf, axis=0, keepdims=True)

    lax.fori_loop(0, nblk, emit_blk, jnp.zeros((1, DSA_TQ), jnp.float32))

    def fill_blk(blk, carry):
        r0 = pl.multiple_of(blk * rb, rb)
        mask_ref[0, pl.ds(r0, rb), :] = jnp.full((rb, DSA_TQ), NEG, jnp.bfloat16)
        return carry

    lax.fori_loop(nblk, (seq + DSA_TK) // rb, fill_blk, 0)


def _dsa_select(ki, qi, wi_t, n_top):
    b, s, _ = ki.shape
    return pl.pallas_call(
        functools.partial(_dsa_select_kernel, n_top=n_top, seq=s),
        grid=(b, s // DSA_TQ),
        in_specs=[
            pl.BlockSpec((1, s, IDX_DIM), lambda i, j: (i, 0, 0)),
            pl.BlockSpec((1, IDX_HEADS, DSA_TQ, IDX_DIM), lambda i, j: (i, 0, j, 0)),
            pl.BlockSpec((1, IDX_HEADS, DSA_TQ), lambda i, j: (i, 0, j)),
        ],
        out_specs=pl.BlockSpec((1, s + DSA_TK, DSA_TQ), lambda i, j: (i, 0, j)),
        out_shape=jax.ShapeDtypeStruct((b, s + DSA_TK, s), jnp.bfloat16),
        scratch_shapes=[pltpu.VMEM((s, DSA_TQ), jnp.int32)],
        compiler_params=_cparams(("parallel", "parallel")),
    )(ki, qi, wi_t)


def _dsa_attn_kernel(q_ref, kv_ref, mask_ref, bres_ref, o_ref,
                     lhs_scr, m_scr, acc_scr, s_a, s_b, p_a, p_b, al_a, al_b, *, n_tiles):
    rows = A_HEADS * DSA_TQ
    qidx = pl.program_id(1)
    nk = (qidx * DSA_TQ + DSA_TQ + DSA_TK - 1) // DSA_TK
    odd = qidx % 2
    n_pairs = (nk + 1) // 2

    lhs_scr[:, :LANES] = q_ref[0].reshape(rows, LANES)
    r = lax.broadcasted_iota(jnp.int32, (rows, DSA_TQ), 0)
    c = lax.broadcasted_iota(jnp.int32, (rows, DSA_TQ), 1)
    lhs_scr[:, LANES:] = jnp.where((r & (DSA_TQ - 1)) == c, 1.0, 0.0).astype(jnp.bfloat16)
    m_scr[...] = jnp.full(m_scr.shape, -jnp.inf, jnp.float32)
    acc_scr[...] = jnp.zeros(acc_scr.shape, jnp.float32)
    p_b[...] = jnp.zeros(p_b.shape, jnp.bfloat16)
    al_b[...] = jnp.ones(al_b.shape, jnp.float32)

    def qk_stage(t, s_out):
        k0 = pl.multiple_of(jnp.minimum(t, n_tiles - 1) * DSA_TK, DSA_TK)
        m0 = pl.multiple_of(t * DSA_TK, DSA_TK)
        rhs = jnp.concatenate([kv_ref[0, pl.ds(k0, DSA_TK), :LANES], mask_ref[0, pl.ds(m0, DSA_TK), :]], axis=1)
        s_out[...] = lax.dot_general(lhs_scr[...], rhs, _NT, preferred_element_type=jnp.float32)

    def softmax_stage(t, s_in, p_out, al_out):
        cfg = jnp.where(t == nk - 1, odd, jnp.where((t == nk - 2) & (odd == 0), 2, 3))
        for h in range(A_HEADS):
            r0 = h * DSA_TQ
            s = s_in[r0:r0 + DSA_TQ, :] + bres_ref[cfg, r0:r0 + DSA_TQ, :]
            m_old = m_scr[r0:r0 + DSA_TQ, :]
            m_new = jnp.maximum(m_old, jnp.max(s, axis=1, keepdims=True))
            al_out[r0:r0 + DSA_TQ, :] = jnp.exp2(m_old - m_new)
            p_out[r0:r0 + DSA_TQ, :] = jnp.exp2((s - jnp.concatenate([m_new, m_new], axis=1)).astype(jnp.bfloat16))
            m_scr[r0:r0 + DSA_TQ, :] = m_new

    def pv_stage(t, p_in, al_in):
        v0 = pl.multiple_of(jnp.clip(t, 0, n_tiles - 1) * DSA_TK, DSA_TK)
        pv = jnp.dot(p_in[...], kv_ref[0, pl.ds(v0, DSA_TK), LANES:], preferred_element_type=jnp.float32)
        acc_scr[...] = acc_scr[...] * al_in[...] + pv

    qk_stage(0, s_a)

    def pair_body(jj, carry):
        t = 2 * jj
        qk_stage(t + 1, s_b)
        softmax_stage(t, s_a, p_a, al_a)
        pv_stage(t - 1, p_b, al_b)
        qk_stage(t + 2, s_a)
        softmax_stage(t + 1, s_b, p_b, al_b)
        pv_stage(t, p_a, al_a)
        return carry

    lax.fori_loop(0, n_pairs, pair_body, 0)
    pv_stage(2 * n_pairs - 1, p_b, al_b)

    for h in range(A_HEADS):
        a = acc_scr[h * DSA_TQ:(h + 1) * DSA_TQ, :]
        o = a[:, :A_HEAD_DIM] / a[:, A_HEAD_DIM:A_HEAD_DIM + 1]
        o_ref[0, :, h * A_HEAD_DIM:(h + 1) * A_HEAD_DIM] = o.astype(jnp.bfloat16)


def _dsa_attn(q_pad, kv, mask_t, bres):
    b, _, s, _ = q_pad.shape
    rows = A_HEADS * DSA_TQ
    return pl.pallas_call(
        functools.partial(_dsa_attn_kernel, n_tiles=s // DSA_TK),
        grid=(b, s // DSA_TQ),
        in_specs=[
            pl.BlockSpec((1, A_HEADS, DSA_TQ, LANES), lambda i, j: (i, 0, j, 0)),
            pl.BlockSpec((1, s, 2 * LANES), lambda i, j: (i, 0, 0)),
            pl.BlockSpec((1, s + DSA_TK, DSA_TQ), lambda i, j: (i, 0, j)),
            pl.BlockSpec(bres.shape, lambda i, j: (0, 0, 0)),
        ],
        out_specs=pl.BlockSpec((1, DSA_TQ, A_HEADS * A_HEAD_DIM), lambda i, j: (i, j, 0)),
        out_shape=jax.ShapeDtypeStruct((b, s, A_HEADS * A_HEAD_DIM), jnp.bfloat16),
        scratch_shapes=[
            pltpu.VMEM((rows, 2 * LANES), jnp.bfloat16),
            pltpu.VMEM((rows, LANES), jnp.float32),
            pltpu.VMEM((rows, LANES), jnp.float32),
            pltpu.VMEM((rows, DSA_TK), jnp.float32), pltpu.VMEM((rows, DSA_TK), jnp.float32),
            pltpu.VMEM((rows, DSA_TK), jnp.bfloat16), pltpu.VMEM((rows, DSA_TK), jnp.bfloat16),
            pltpu.VMEM((rows, LANES), jnp.float32), pltpu.VMEM((rows, LANES), jnp.float32),
        ],
        compiler_params=_cparams(("parallel", "parallel")),
    )(q_pad, kv, mask_t, bres)


def _out_proj_kernel(h_ref, a_ref, w_ref, o_ref):
    o_ref[...] = h_ref[...] + jnp.dot(a_ref[...], w_ref[...], preferred_element_type=jnp.float32)


def _out_proj_residual(h2d, a2d, w):
    n, d = h2d.shape
    tm = min(ROW_TILE, n)
    return pl.pallas_call(
        _out_proj_kernel,
        grid=(n // tm,),
        in_specs=[
            pl.BlockSpec((tm, d), lambda i: (i, 0)),
            pl.BlockSpec((tm, a2d.shape[1]), lambda i: (i, 0)),
            pl.BlockSpec(w.shape, lambda i: (0, 0)),
        ],
        out_specs=pl.BlockSpec((tm, d), lambda i: (i, 0)),
        out_shape=jax.ShapeDtypeStruct((n, d), jnp.float32),
        compiler_params=_cparams(("parallel",)),
    )(h2d, a2d, w)


def _mlp_kernel(h_ref, g_ref, wu_ref, wd_ref, o_ref, *, tf):
    x = h_ref[...]
    xn = _rms(x, g_ref[...]).astype(jnp.bfloat16)
    o_ref[...] = x
    for f in range(0, D_FF, tf):
        u = jnp.dot(xn, wu_ref[:, f:f + tf], preferred_element_type=jnp.float32)
        a = jnp.square(jnp.maximum(u, 0.0)).astype(jnp.bfloat16)
        o_ref[...] += jnp.dot(a, wd_ref[f:f + tf, :], preferred_element_type=jnp.float32)


def _mlp(h2d, g, wu, wd):
    n, d = h2d.shape
    tm = min(ROW_TILE, n)
    resident = lambda shape: pl.BlockSpec(shape, lambda i: (0, 0), pipeline_mode=pl.Buffered(1))
    return pl.pallas_call(
        functools.partial(_mlp_kernel, tf=1024),
        grid=(n // tm,),
        in_specs=[
            pl.BlockSpec((tm, d), lambda i: (i, 0)),
            pl.BlockSpec((1, d), lambda i: (0, 0)),
            resident(wu.shape), resident(wd.shape),
        ],
        out_specs=pl.BlockSpec((tm, d), lambda i: (i, 0)),
        out_shape=jax.ShapeDtypeStruct((n, d), jnp.float32),
        compiler_params=_cparams(("parallel",)),
    )(h2d, g, wu, wd)


def _rope_padded(x, cos_f, sin_a, sin_b):
    return x * cos_f + pltpu.roll(x, LANES - QK_ROPE // 2, 1) * sin_a + pltpu.roll(x, QK_ROPE // 2, 1) * sin_b


def _mla_kv_kernel(h_ref, g_ref, wc_ref, wr_ref, gl_ref, wk_ref, wv_ref, gk_ref, cos_ref, sa_ref, sb_ref,
                   k_ref, v_ref):
    xn = _rms(h_ref[0], g_ref[...]).astype(jnp.bfloat16)
    c = jnp.dot(xn, wc_ref[...], preferred_element_type=jnp.float32)
    c = _rms(c, gl_ref[...]).astype(jnp.bfloat16)
    k_rope = jnp.dot(xn, wr_ref[...], preferred_element_type=jnp.float32)
    kn = jnp.dot(c, wk_ref[...], preferred_element_type=jnp.float32)
    vv = jnp.dot(c, wv_ref[...], preferred_element_type=jnp.float32)
    lane = lax.broadcasted_iota(jnp.int32, (1, LANES), 1)
    v_ones = jnp.where(lane == V_DIM, 1.0, 0.0)
    cos_f, sin_a, sin_b = cos_ref[...], sa_ref[...], sb_ref[...]
    for h in range(B_HEADS):
        kh = kn[:, h * LANES:(h + 1) * LANES] + k_rope
        kh = _rope_padded(_rms_padded(kh, gk_ref[...], QK_DIM), cos_f, sin_a, sin_b)
        k_ref[0, h] = kh.astype(jnp.bfloat16)
        v_ref[0, h] = (vv[:, h * LANES:(h + 1) * LANES] + v_ones).astype(jnp.bfloat16)


def _mla_kv_prep(h, g, wc, wr, gl, wk, wv, gk, cos_f, sin_a, sin_b):
    b, s, d = h.shape
    tm = min(ROW_TILE, s)
    const = lambda shape: pl.BlockSpec(shape, lambda i, j: (0,) * len(shape))
    pos = pl.BlockSpec((tm, LANES), lambda i, j: (j, 0))
    head_out = pl.BlockSpec((1, B_HEADS, tm, LANES), lambda i, j: (i, 0, j, 0))
    return pl.pallas_call(
        _mla_kv_kernel,
        grid=(b, s // tm),
        in_specs=[
            pl.BlockSpec((1, tm, d), lambda i, j: (i, j, 0)),
            const((1, d)), const(wc.shape), const(wr.shape), const((1, KV_LORA)), const(wk.shape),
            const(wv.shape), const((1, LANES)), pos, pos, pos,
        ],
        out_specs=[head_out, head_out],
        out_shape=[jax.ShapeDtypeStruct((b, B_HEADS, s, LANES), jnp.bfloat16)] * 2,
        compiler_params=_cparams(("parallel", "parallel")),
    )(h, g, wc, wr, gl, wk, wv, gk, cos_f, sin_a, sin_b)


def _mla_q_kernel(h_ref, g_ref, wd_ref, gl_ref, wu_ref, gq_ref, cos_ref, sa_ref, sb_ref, q_ref):
    xn = _rms(h_ref[0], g_ref[...]).astype(jnp.bfloat16)
    c = jnp.dot(xn, wd_ref[...], preferred_element_type=jnp.float32)
    c = _rms(c, gl_ref[...]).astype(jnp.bfloat16)
    yq = jnp.dot(c, wu_ref[...], preferred_element_type=jnp.float32)
    cos_f, sin_a, sin_b = cos_ref[...], sa_ref[...], sb_ref[...]
    for h in range(B_HEADS):
        qh = _rms_padded(yq[:, h * LANES:(h + 1) * LANES], gq_ref[...], QK_DIM)
        q_ref[0, h] = _rope_padded(qh, cos_f, sin_a, sin_b).astype(jnp.bfloat16)


def _mla_q_prep(h, g, wd, gl, wu, gq, cos_f, sin_a, sin_b):
    b, s, d = h.shape
    tm = min(ROW_TILE, s)
    const = lambda shape: pl.BlockSpec(shape, lambda i, j: (0,) * len(shape))
    pos = pl.BlockSpec((tm, LANES), lambda i, j: (j, 0))
    return pl.pallas_call(
        _mla_q_kernel,
        grid=(b, s // tm),
        in_specs=[
            pl.BlockSpec((1, tm, d), lambda i, j: (i, j, 0)),
            const((1, d)), const(wd.shape), const((1, Q_LORA)), const(wu.shape), const((1, LANES)),
            pos, pos, pos,
        ],
        out_specs=pl.BlockSpec((1, B_HEADS, tm, LANES), lambda i, j: (i, 0, j, 0)),
        out_shape=jax.ShapeDtypeStruct((b, B_HEADS, s, LANES), jnp.bfloat16),
        compiler_params=_cparams(("parallel", "parallel")),
    )(h, g, wd, gl, wu, gq, cos_f, sin_a, sin_b)


def _mla_attn_kernel(q_ref, k_ref, v_ref, o_ref, m_scr, acc_scr, s_0, s_1, p_0, p_1, al_0, al_1, *, tq):
    diag = pl.program_id(2)
    m_scr[...] = jnp.full(m_scr.shape, -jnp.inf, jnp.float32)
    acc_scr[...] = jnp.zeros(acc_scr.shape, jnp.float32)
    p_1[...] = jnp.zeros(p_1.shape, jnp.bfloat16)
    al_1[...] = jnp.ones(al_1.shape, jnp.float32)

    def qk_stage(j, hh, s_out):
        k0 = pl.multiple_of(j * tq, tq)
        s_out[...] = lax.dot_general(q_ref[0, hh], k_ref[0, hh, pl.ds(k0, tq), :], _NT,
                                     preferred_element_type=jnp.float32)

    def softmax_stage(hh, s_in, p_out, al_out, masked):
        for g0 in range(0, tq, MLA_ROW_GROUP):
            g1 = g0 + MLA_ROW_GROUP
            s = s_in[g0:g1, :]
            if masked:
                q_chunk = (g0 + lax.broadcasted_iota(jnp.int32, (MLA_ROW_GROUP, tq), 0)) >> CHUNK_SHIFT
                k_chunk = lax.broadcasted_iota(jnp.int32, (MLA_ROW_GROUP, tq), 1) >> CHUNK_SHIFT
                s = jnp.where(k_chunk <= q_chunk, s, NEG)
            m_old = m_scr[hh, g0:g1, :]
            m_new = jnp.maximum(m_old, jnp.max(s, axis=1, keepdims=True))
            al_out[g0:g1, :] = jnp.exp2(m_old - m_new)
            p_out[g0:g1, :] = jnp.exp2((s - jnp.concatenate([m_new] * (tq // LANES), axis=1)).astype(jnp.bfloat16))
            m_scr[hh, g0:g1, :] = m_new

    def pv_stage(j, hh, p_in, al_in):
        v0 = pl.multiple_of(jnp.maximum(j, 0) * tq, tq)
        pv = jnp.dot(p_in[...], v_ref[0, hh, pl.ds(v0, tq), :], preferred_element_type=jnp.float32)
        acc_scr[hh] = acc_scr[hh] * al_in[...] + pv

    qk_stage(0, 0, s_0)

    def full_tile(j, carry):
        qk_stage(j, 1, s_1)
        softmax_stage(0, s_0, p_0, al_0, False)
        pv_stage(j - 1, 1, p_1, al_1)
        qk_stage(j + 1, 0, s_0)
        softmax_stage(1, s_1, p_1, al_1, False)
        pv_stage(j, 0, p_0, al_0)
        return carry

    lax.fori_loop(0, diag, full_tile, 0)
    qk_stage(diag, 1, s_1)
    softmax_stage(0, s_0, p_0, al_0, True)
    pv_stage(diag - 1, 1, p_1, al_1)
    softmax_stage(1, s_1, p_1, al_1, True)
    pv_stage(diag, 0, p_0, al_0)
    pv_stage(diag, 1, p_1, al_1)
    outs = []
    for hh in range(2):
        a = acc_scr[hh]
        outs.append(a[:, :V_DIM] / a[:, V_DIM:V_DIM + 1])
    o_ref[0] = jnp.concatenate(outs, axis=1).astype(jnp.bfloat16)


def _mla_attn(q_pad, k_pad, v_aug):
    b, nh, s, _ = q_pad.shape
    tq = min(MLA_TQ, s)
    return pl.pallas_call(
        functools.partial(_mla_attn_kernel, tq=tq),
        grid=(b, nh // 2, s // tq),
        in_specs=[
            pl.BlockSpec((1, 2, tq, LANES), lambda i, p, j: (i, p, j, 0)),
            pl.BlockSpec((1, 2, s, LANES), lambda i, p, j: (i, p, 0, 0)),
            pl.BlockSpec((1, 2, s, LANES), lambda i, p, j: (i, p, 0, 0)),
        ],
        out_specs=pl.BlockSpec((1, tq, 2 * V_DIM), lambda i, p, j: (i, j, p)),
        out_shape=jax.ShapeDtypeStruct((b, s, nh * V_DIM), jnp.bfloat16),
        scratch_shapes=[
            pltpu.VMEM((2, tq, LANES), jnp.float32), pltpu.VMEM((2, tq, LANES), jnp.float32),
            pltpu.VMEM((tq, tq), jnp.float32), pltpu.VMEM((tq, tq), jnp.float32),
            pltpu.VMEM((tq, tq), jnp.bfloat16), pltpu.VMEM((tq, tq), jnp.bfloat16),
            pltpu.VMEM((tq, LANES), jnp.float32), pltpu.VMEM((tq, LANES), jnp.float32),
        ],
        compiler_params=_cparams(("parallel", "parallel", "parallel")),
    )(q_pad, k_pad, v_aug)


def _t5_bucket(rel):
    nb = N_BUCKETS // 2
    max_exact = nb // 2
    ret = jnp.where(rel > 0, nb, 0)
    n = jnp.abs(rel)
    nf = jnp.maximum(n, 1).astype(jnp.float32)
    large = max_exact + (jnp.log(nf / max_exact) / math.log(MAX_DISTANCE / max_exact)
                         * (nb - max_exact)).astype(jnp.int32)
    large = jnp.minimum(large, nb - 1)
    return ret + jnp.where(n < max_exact, n, large)


def _pad_lanes(a, width=LANES):
    return jnp.pad(a, [(0, 0)] * (a.ndim - 1) + [(0, width - a.shape[-1])])


def _pad_heads(w, n_heads, head_dim):
    k = w.shape[0]
    return _pad_lanes(w.reshape(k, n_heads, head_dim)).reshape(k, n_heads * LANES)


def _bias_tables(rel_bias):
    far = rel_bias[N_BUCKETS // 2 - 1].astype(jnp.float32) * LOG2E
    c_hi = far.astype(jnp.bfloat16)
    c_lo = (far - c_hi.astype(jnp.float32)).astype(jnp.bfloat16)
    far_eff = c_hi.astype(jnp.float32) + c_lo.astype(jnp.float32)
    cb = jnp.zeros((A_HEADS, LANES), jnp.float32)
    cb = cb.at[:, A_HEAD_DIM].set(c_hi.astype(jnp.float32)).at[:, A_HEAD_DIM + 1].set(c_lo.astype(jnp.float32))
    r = jnp.arange(DSA_TQ, dtype=jnp.int32)[:, None]
    c = jnp.arange(DSA_TK, dtype=jnp.int32)[None, :]
    table = rel_bias.astype(jnp.float32) * LOG2E - far_eff[None, :]
    tiles = []
    for off in (0, DSA_TQ, DSA_TK):
        onehot = jax.nn.one_hot(_t5_bucket(c - r - off), N_BUCKETS, dtype=jnp.float32)
        tiles.append(jnp.einsum('rcb,bh->hrc', onehot, table, precision=lax.Precision.HIGHEST))
    tiles.append(jnp.zeros_like(tiles[0]))
    bres = jnp.stack(tiles).reshape(4, A_HEADS * DSA_TQ, DSA_TK)
    return cb, bres


def _rope_tables_padded(seq_len):
    pos = jnp.arange(seq_len, dtype=jnp.float32)
    inv_freq = 1.0 / (ROPE_THETA ** (jnp.arange(0, QK_ROPE, 2, dtype=jnp.float32) / QK_ROPE))
    ang = pos[:, None] * inv_freq[None, :]
    cos, sin = jnp.cos(ang), jnp.sin(ang)
    half = QK_ROPE // 2
    ones = jnp.ones((seq_len, QK_NOPE), jnp.float32)
    zeros = jnp.zeros((seq_len, QK_NOPE), jnp.float32)
    zh = jnp.zeros((seq_len, half), jnp.float32)
    cos_f = _pad_lanes(jnp.concatenate([ones, cos, cos], axis=1))
    sin_a = _pad_lanes(jnp.concatenate([zeros, -sin, zh], axis=1))
    sin_b = _pad_lanes(jnp.concatenate([zeros, zh, sin], axis=1))
    return cos_f, sin_a, sin_b


def kernel(x, rel_bias, a_attn_norm, a_w_in, a_q_norm, a_k_norm, a_w_o, kv_norm, w_dkv, kv_lora_norm, w_ukv, k_norm, b_attn_norm, b_w_dq, b_q_lora_norm, b_w_uq, b_q_norm, b_w_o, mlp_norm, mlp_w_up, mlp_w_down):
    b, s, d = x.shape
    assert d == D_MODEL and s % DSA_TK == 0 and s % min(MLA_TQ, s) == 0
    n_top = min(TOPK_MAX, s // 4)
    bf = jnp.bfloat16
    n_a = a_w_in.shape[0]
    n_b = b_w_dq.shape[0]
    cb, bres = _bias_tables(rel_bias)
    cos_f, sin_a, sin_b = _rope_tables_padded(s)
    row = lambda v: v.reshape(1, -1).astype(jnp.float32)

    o1 = A_HEADS * A_HEAD_DIM
    o2 = o1 + A_HEAD_DIM
    o3 = o2 + A_HEAD_DIM
    o4 = o3 + IDX_HEADS * IDX_DIM
    o5 = o4 + IDX_DIM

    h = x
    layer = 0
    for i in range(n_a):
        w_in = a_w_in[i]
        wq = _pad_heads(w_in[:, :o1], A_HEADS, A_HEAD_DIM).astype(bf)
        wr = jnp.concatenate([_pad_lanes(w_in[:, o1:o2]), _pad_lanes(w_in[:, o2:o3]), w_in[:, o3:o4],
                              _pad_lanes(w_in[:, o4:])], axis=1).astype(bf)
        gq = _pad_lanes(row(a_q_norm[i]) * (A_HEAD_DIM ** -0.5 * LOG2E))
        gk = _pad_lanes(row(a_k_norm[i]))
        q_pad, kv, qi, misc = _dsa_proj(h, row(a_attn_norm[i]), wq, wr, gq, gk, cb)
        ki = misc[..., :IDX_DIM].astype(bf)
        wi_t = jnp.swapaxes(misc[..., IDX_DIM:IDX_DIM + IDX_HEADS] * (IDX_HEADS ** -0.5), 1, 2)
        mask_t = _dsa_select(ki, qi, wi_t, n_top)
        attn = _dsa_attn(q_pad, kv, mask_t, bres)
        h2 = _out_proj_residual(h.reshape(b * s, d), attn.reshape(b * s, -1), a_w_o[i].astype(bf))
        h2 = _mlp(h2, row(mlp_norm[layer]), mlp_w_up[layer].astype(bf), mlp_w_down[layer].astype(bf))
        h = h2.reshape(b, s, d)
        layer += 1

    w_ukv3 = w_ukv.reshape(KV_LORA, B_HEADS, QK_NOPE + V_DIM)
    wk = _pad_lanes(w_ukv3[:, :, :QK_NOPE]).reshape(KV_LORA, B_HEADS * LANES).astype(bf)
    wv = _pad_lanes(w_ukv3[:, :, QK_NOPE:]).reshape(KV_LORA, B_HEADS * LANES).astype(bf)
    w_kr = jnp.pad(w_dkv[:, KV_LORA:], ((0, 0), (QK_NOPE, LANES - QK_DIM))).astype(bf)
    k_pad, v_aug = _mla_kv_prep(h, row(kv_norm), w_dkv[:, :KV_LORA].astype(bf), w_kr, row(kv_lora_norm),
                                wk, wv, _pad_lanes(row(k_norm)), cos_f, sin_a, sin_b)
    for j in range(n_b):
        wu = _pad_heads(b_w_uq[j], B_HEADS, QK_DIM).astype(bf)
        gq = _pad_lanes(row(b_q_norm[j]) * (QK_DIM ** -0.5 * LOG2E))
        q_pad = _mla_q_prep(h, row(b_attn_norm[j]), b_w_dq[j].astype(bf), row(b_q_lora_norm[j]), wu, gq,
                            cos_f, sin_a, sin_b)
        attn = _mla_attn(q_pad, k_pad, v_aug)
        h2 = _out_proj_residual(h.reshape(b * s, d), attn.reshape(b * s, -1), b_w_o[j].astype(bf))
        h2 = _mlp(h2, row(mlp_norm[layer]), mlp_w_up[layer].astype(bf), mlp_w_down[layer].astype(bf))
        h = h2.reshape(b, s, d)
        layer += 1
    return h
```

```python
import functools
import math

import numpy as np
import jax
import jax.numpy as jnp
from jax import lax
from jax.experimental import pallas as pl
from jax.experimental.pallas import tpu as pltpu

D_MODEL = 1024
CHUNK = 64
CHUNK_SHIFT = 6
EPS = 1e-6
NEG = -1e30
LOG2E = 1.4426950408889634

A_HEADS = 16
A_HEAD_DIM = 64
IDX_HEADS = 8
IDX_DIM = 64
TOPK_MAX = 256
N_BUCKETS = 32
MAX_DISTANCE = 128

B_HEADS = 16
QK_NOPE = 64
QK_ROPE = 32
QK_DIM = QK_NOPE + QK_ROPE
V_DIM = 64
KV_LORA = 256
Q_LORA = 384
ROPE_THETA = 10000.0
D_FF = 4 * D_MODEL

LANES = 128
MIN_NORMAL_HI = 0x0080
SENTINEL_SCORE = -3.0e38
_SENTINEL_BITS = int(np.float32(SENTINEL_SCORE).view(np.int32))
SENTINEL_KEY = _SENTINEL_BITS ^ ((_SENTINEL_BITS >> 31) & 0x7FFFFFFF)
VMEM_LIMIT = 56 * 1024 * 1024

ROW_TILE = 512
DSA_TQ = 128
DSA_TK = 256
SEL_RB = 256
MLA_TQ = 512
MLA_ROW_GROUP = 128

_NT = (((1,), (1,)), ((), ()))


def _cparams(sem):
    return pltpu.CompilerParams(dimension_semantics=sem, vmem_limit_bytes=VMEM_LIMIT)


def _rms(x, g):
    return x * lax.rsqrt(jnp.mean(x * x, axis=-1, keepdims=True) + EPS) * g


def _rms_padded(x, g, true_dim):
    ms = jnp.sum(x * x, axis=-1, keepdims=True) * (1.0 / true_dim)
    return x * lax.rsqrt(ms + EPS) * g


def _dsa_proj_kernel(h_ref, g_ref, wq_ref, wr_ref, gq_ref, gk_ref, cb_ref, q_ref, kv_ref, qi_ref, misc_ref):
    xn = _rms(h_ref[0], g_ref[...]).astype(jnp.bfloat16)
    yq = jnp.dot(xn, wq_ref[...], preferred_element_type=jnp.float32)
    for h in range(A_HEADS):
        qh = yq[:, h * LANES:(h + 1) * LANES]
        qn = _rms_padded(qh, gq_ref[...], A_HEAD_DIM) + cb_ref[h:h + 1, :]
        q_ref[0, h] = qn.astype(jnp.bfloat16)
    yr = jnp.dot(xn, wr_ref[...], preferred_element_type=jnp.float32)
    lane = lax.broadcasted_iota(jnp.int32, (1, LANES), 1)
    k_ones = jnp.where((lane == A_HEAD_DIM) | (lane == A_HEAD_DIM + 1), 1.0, 0.0)
    v_ones = jnp.where(lane == A_HEAD_DIM, 1.0, 0.0)
    kn = _rms_padded(yr[:, :LANES], gk_ref[...], A_HEAD_DIM) + k_ones
    vv = yr[:, LANES:2 * LANES] + v_ones
    kv_ref[0, :, :LANES] = kn.astype(jnp.bfloat16)
    kv_ref[0, :, LANES:] = vv.astype(jnp.bfloat16)
    qi = yr[:, 2 * LANES:2 * LANES + IDX_HEADS * IDX_DIM] * (IDX_DIM ** -0.5)
    for h in range(IDX_HEADS):
        qi_ref[0, h] = qi[:, h * IDX_DIM:(h + 1) * IDX_DIM].astype(jnp.bfloat16)
    misc_ref[0] = yr[:, 2 * LANES + IDX_HEADS * IDX_DIM:]


def _dsa_proj(h, g, wq, wr, gq, gk, cb):
    b, s, d = h.shape
    tm = min(ROW_TILE, s)
    const = lambda shape: pl.BlockSpec(shape, lambda i, j: (0,) * len(shape))
    return pl.pallas_call(
        _dsa_proj_kernel,
        grid=(b, s // tm),
        in_specs=[
            pl.BlockSpec((1, tm, d), lambda i, j: (i, j, 0)),
            const((1, d)), const(wq.shape), const(wr.shape), const((1, LANES)), const((1, LANES)),
            const((A_HEADS, LANES)),
        ],
        out_specs=[
            pl.BlockSpec((1, A_HEADS, tm, LANES), lambda i, j: (i, 0, j, 0)),
            pl.BlockSpec((1, tm, 2 * LANES), lambda i, j: (i, j, 0)),
            pl.BlockSpec((1, IDX_HEADS, tm, IDX_DIM), lambda i, j: (i, 0, j, 0)),
            pl.BlockSpec((1, tm, LANES), lambda i, j: (i, j, 0)),
        ],
        out_shape=[
            jax.ShapeDtypeStruct((b, A_HEADS, s, LANES), jnp.bfloat16),
            jax.ShapeDtypeStruct((b, s, 2 * LANES), jnp.bfloat16),
            jax.ShapeDtypeStruct((b, IDX_HEADS, s, IDX_DIM), jnp.bfloat16),
            jax.ShapeDtypeStruct((b, s, LANES), jnp.float32),
        ],
        compiler_params=_cparams(("parallel", "parallel")),
    )(h, g, wq, wr, gq, gk, cb)


def _dsa_select_kernel(ki_ref, qi_ref, wi_ref, mask_ref, keys_scr, hi_scr, *, n_top, seq):
    rb = SEL_RB
    t0 = pl.program_id(1) * DSA_TQ
    nblk = (t0 + DSA_TQ + rb - 1) // rb
    qi2d = qi_ref[0].reshape(IDX_HEADS * DSA_TQ, IDX_DIM)
    w = wi_ref[0]
    t_chunk = (t0 + lax.broadcasted_iota(jnp.int32, (1, DSA_TQ), 1)) >> CHUNK_SHIFT
    row_iota = lax.broadcasted_iota(jnp.int32, (rb, DSA_TQ), 0)

    def admissible(r0):
        return ((r0 + row_iota) >> CHUNK_SHIFT) <= t_chunk

    def idx_dots(blk):
        r0 = pl.multiple_of(jnp.minimum(blk, seq // rb - 1) * rb, rb)
        return lax.dot_general(ki_ref[0, pl.ds(r0, rb), :], qi2d, _NT, preferred_element_type=jnp.float32)

    def store_keys(blk, d):
        r0 = pl.multiple_of(blk * rb, rb)
        acc = jnp.zeros((rb, DSA_TQ), jnp.float32)
        for h in range(IDX_HEADS):
            acc = acc + w[h:h + 1, :] * jnp.maximum(d[:, h * DSA_TQ:(h + 1) * DSA_TQ], 0.0)
        bits = lax.bitcast_convert_type(jnp.where(admissible(r0), acc, SENTINEL_SCORE), jnp.int32)
        keys_scr[pl.ds(r0, rb), :] = bits ^ ((bits >> 31) & 0x7FFFFFFF)
        hi_scr[pl.ds(r0, rb), :] = lax.bitcast_convert_type(bits & -65536, jnp.float32).astype(jnp.bfloat16)

    def score_pair(pp, carry):
        d0 = idx_dots(2 * pp)
        d1 = idx_dots(2 * pp + 1)
        store_keys(2 * pp, d0)
        store_keys(2 * pp + 1, d1)
        return carry

    lax.fori_loop(0, (nblk + 1) // 2, score_pair, 0)

    cb = 2 * rb
    n_cnt = (nblk + 1) // 2

    def count(pred_fn):
        def blk_body(step, cnt):
            r0 = pl.multiple_of(step * cb, cb)
            hit = jnp.where(pred_fn(keys_scr[pl.ds(r0, cb), :]), 1, 0)
            return cnt + jnp.sum(hit.reshape(cb // 8, 8, DSA_TQ), axis=0)
        cnt8 = lax.fori_loop(0, n_cnt, blk_body, jnp.zeros((8, DSA_TQ), jnp.int32))
        return jnp.sum(cnt8, axis=0, keepdims=True)

    def count_hi(trial_b):
        one, zero = jnp.ones((), jnp.bfloat16), jnp.zeros((), jnp.bfloat16)

        def blk_body(step, cnt):
            r0 = pl.multiple_of(step * cb, cb)
            hit = jnp.where(hi_scr[pl.ds(r0, cb), :] >= trial_b, one, zero)
            parts = [hit[r:r + 16, :] for r in range(0, cb, 16)]
            while len(parts) > 1:
                parts = [parts[i] + parts[i + 1] for i in range(0, len(parts), 2)]
            return cnt + parts[0].astype(jnp.float32)
        cnt16 = lax.fori_loop(0, n_cnt, blk_body, jnp.zeros((16, DSA_TQ), jnp.float32))
        return jnp.sum(cnt16, axis=0, keepdims=True)

    def hi_pass(i, carry):
        thr_u, cnt_thr = carry
        trial_u = thr_u + lax.shift_left(jnp.int32(1), 15 - i)
        k = trial_u - 32768
        k = jnp.where((k > 0) & (k < MIN_NORMAL_HI), MIN_NORMAL_HI, k)
        trial_bits = lax.shift_left(k ^ ((k >> 31) & 0x7FFF), 16)
        trial_b = lax.bitcast_convert_type(trial_bits, jnp.float32).astype(jnp.bfloat16)
        cnt = count_hi(trial_b)
        ok = cnt >= n_top
        return jnp.where(ok, trial_u, thr_u), jnp.where(ok, cnt, cnt_thr)

    thr_u, cnt_thr = lax.fori_loop(0, 16, hi_pass, (jnp.zeros((1, DSA_TQ), jnp.int32),
                                                    jnp.full((1, DSA_TQ), n_top, jnp.float32)))

    def lo_pass(i, carry):
        thr, cnt_thr = carry
        trial = thr + lax.shift_left(jnp.int32(1), 15 - i)
        cnt = count(lambda k: k >= trial).astype(jnp.float32)
        ok = cnt >= n_top
        return jnp.where(ok, trial, thr), jnp.where(ok, cnt, cnt_thr)

    thr, cnt_thr = lax.fori_loop(0, 16, lo_pass, (lax.shift_left(thr_u - 32768, 16), cnt_thr))
    has_ties = jnp.max(jnp.where((cnt_thr != n_top) & (thr != SENTINEL_KEY), 1, 0)) > 0

    @pl.when(jnp.logical_not(has_ties))
    def _():
        def emit_blk(blk, carry):
            r0 = pl.multiple_of(blk * rb, rb)
            sel = (keys_scr[pl.ds(r0, rb), :] >= thr) & admissible(r0)
            mask_ref[0, pl.ds(r0, rb), :] = jnp.where(sel, 0.0, NEG).astype(jnp.bfloat16)
            return carry

        lax.fori_loop(0, nblk, emit_blk, 0)

    @pl.when(has_ties)
    def _():
        quota = (n_top - count(lambda k: k > thr)).astype(jnp.float32)
        tri = (lax.broadcasted_iota(jnp.int32, (rb, rb), 1)
               < lax.broadcasted_iota(jnp.int32, (rb, rb), 0))
        tri = jnp.where(tri, 1.0, 0.0).astype(jnp.bfloat16)

        def emit_blk(blk, seen):
            r0 = pl.multiple_of(blk * rb, rb)
            key = keys_scr[pl.ds(r0, rb), :]
            eq = key == thr
            eqf = jnp.where(eq, 1.0, 0.0)
            before = jnp.dot(tri, eqf.astype(jnp.bfloat16), preferred_element_type=jnp.float32) + seen
            sel = (key > thr) | (eq & (before < quota))
            sel = sel & admissible(r0)
            mask_ref[0, pl.ds(r0, rb), :] = jnp.where(sel, 0.0, NEG).astype(jnp.bfloat16)
            return seen + jnp.sum(eqf, axis=0, keepdims=True)

        lax.fori_loop(0, nblk, emit_blk, jnp.zeros((1, DSA_TQ), jnp.float32))

    def fill_blk(blk, carry):
        r0 = pl.multiple_of(blk * rb, rb)
        mask_ref[0, pl.ds(r0, rb), :] = jnp.full((rb, DSA_TQ), NEG, jnp.bfloat16)
        return carry

    lax.fori_loop(nblk, (seq + DSA_TK) // rb, fill_blk, 0)


def _dsa_select(ki, qi, wi_t, n_top):
    b, s, _ = ki.shape
    return pl.pallas_call(
        functools.partial(_dsa_select_kernel, n_top=n_top, seq=s),
        grid=(b, s // DSA_TQ),
        in_specs=[
            pl.BlockSpec((1, s, IDX_DIM), lambda i, j: (i, 0, 0)),
            pl.BlockSpec((1, IDX_HEADS, DSA_TQ, IDX_DIM), lambda i, j: (i, 0, j, 0)),
            pl.BlockSpec((1, IDX_HEADS, DSA_TQ), lambda i, j: (i, 0, j)),
        ],
        out_specs=pl.BlockSpec((1, s + DSA_TK, DSA_TQ), lambda i, j: (i, 0, j)),
        out_shape=jax.ShapeDtypeStruct((b, s + DSA_TK, s), jnp.bfloat16),
        scratch_shapes=[pltpu.VMEM((s + SEL_RB, DSA_TQ), jnp.int32),
                        pltpu.VMEM((s + SEL_RB, DSA_TQ), jnp.bfloat16)],
        compiler_params=_cparams(("parallel", "parallel")),
    )(ki, qi, wi_t)


def _dsa_attn_kernel(q_ref, kv_ref, mask_ref, bres_ref, o_ref,
                     lhs_scr, m_scr, acc_scr, s_a, s_b, p_a, p_b, al_a, al_b, *, n_tiles):
    rows = A_HEADS * DSA_TQ
    qidx = pl.program_id(1)
    nk = (qidx * DSA_TQ + DSA_TQ + DSA_TK - 1) // DSA_TK
    odd = qidx % 2
    n_pairs = (nk + 1) // 2

    lhs_scr[:, :LANES] = q_ref[0].reshape(rows, LANES)
    r = lax.broadcasted_iota(jnp.int32, (rows, DSA_TQ), 0)
    c = lax.broadcasted_iota(jnp.int32, (rows, DSA_TQ), 1)
    lhs_scr[:, LANES:] = jnp.where((r & (DSA_TQ - 1)) == c, 1.0, 0.0).astype(jnp.bfloat16)
    m_scr[...] = jnp.full(m_scr.shape, -jnp.inf, jnp.float32)
    acc_scr[...] = jnp.zeros(acc_scr.shape, jnp.float32)
    p_b[...] = jnp.zeros(p_b.shape, jnp.bfloat16)
    al_b[...] = jnp.ones(al_b.shape, jnp.float32)

    def qk_stage(t, s_out):
        k0 = pl.multiple_of(jnp.minimum(t, n_tiles - 1) * DSA_TK, DSA_TK)
        m0 = pl.multiple_of(t * DSA_TK, DSA_TK)
        rhs = jnp.concatenate([kv_ref[0, pl.ds(k0, DSA_TK), :LANES], mask_ref[0, pl.ds(m0, DSA_TK), :]], axis=1)
        s_out[...] = lax.dot_general(lhs_scr[...], rhs, _NT, preferred_element_type=jnp.float32)

    def softmax_stage(t, s_in, p_out, al_out):
        cfg = jnp.where(t == nk - 1, odd, jnp.where((t == nk - 2) & (odd == 0), 2, 3))
        for h in range(A_HEADS):
            r0 = h * DSA_TQ
            s = s_in[r0:r0 + DSA_TQ, :] + bres_ref[cfg, r0:r0 + DSA_TQ, :]
            m_old = m_scr[r0:r0 + DSA_TQ, :]
            m_new = jnp.maximum(m_old, jnp.max(s, axis=1, keepdims=True))
            al_out[r0:r0 + DSA_TQ, :] = jnp.exp2(m_old - m_new)
            p_out[r0:r0 + DSA_TQ, :] = jnp.exp2((s - jnp.concatenate([m_new, m_new], axis=1)).astype(jnp.bfloat16))
            m_scr[r0:r0 + DSA_TQ, :] = m_new

    def pv_stage(t, p_in, al_in):
        v0 = pl.multiple_of(jnp.clip(t, 0, n_tiles - 1) * DSA_TK, DSA_TK)
        pv = jnp.dot(p_in[...], kv_ref[0, pl.ds(v0, DSA_TK), LANES:], preferred_element_type=jnp.float32)
        acc_scr[...] = acc_scr[...] * al_in[...] + pv

    qk_stage(0, s_a)

    def pair_body(jj, carry):
        t = 2 * jj
        qk_stage(t + 1, s_b)
        softmax_stage(t, s_a, p_a, al_a)
        pv_stage(t - 1, p_b, al_b)
        qk_stage(t + 2, s_a)
        softmax_stage(t + 1, s_b, p_b, al_b)
        pv_stage(t, p_a, al_a)
        return carry

    lax.fori_loop(0, n_pairs, pair_body, 0)
    pv_stage(2 * n_pairs - 1, p_b, al_b)

    for h in range(A_HEADS):
        a = acc_scr[h * DSA_TQ:(h + 1) * DSA_TQ, :]
        o = a[:, :A_HEAD_DIM] / a[:, A_HEAD_DIM:A_HEAD_DIM + 1]
        o_ref[0, :, h * A_HEAD_DIM:(h + 1) * A_HEAD_DIM] = o.astype(jnp.bfloat16)


def _dsa_attn(q_pad, kv, mask_t, bres):
    b, _, s, _ = q_pad.shape
    rows = A_HEADS * DSA_TQ
    return pl.pallas_call(
        functools.partial(_dsa_attn_kernel, n_tiles=s // DSA_TK),
        grid=(b, s // DSA_TQ),
        in_specs=[
            pl.BlockSpec((1, A_HEADS, DSA_TQ, LANES), lambda i, j: (i, 0, j, 0)),
            pl.BlockSpec((1, s, 2 * LANES), lambda i, j: (i, 0, 0)),
            pl.BlockSpec((1, s + DSA_TK, DSA_TQ), lambda i, j: (i, 0, j)),
            pl.BlockSpec(bres.shape, lambda i, j: (0, 0, 0)),
        ],
        out_specs=pl.BlockSpec((1, DSA_TQ, A_HEADS * A_HEAD_DIM), lambda i, j: (i, j, 0)),
        out_shape=jax.ShapeDtypeStruct((b, s, A_HEADS * A_HEAD_DIM), jnp.bfloat16),
        scratch_shapes=[
            pltpu.VMEM((rows, 2 * LANES), jnp.bfloat16),
            pltpu.VMEM((rows, LANES), jnp.float32),
            pltpu.VMEM((rows, LANES), jnp.float32),
            pltpu.VMEM((rows, DSA_TK), jnp.float32), pltpu.VMEM((rows, DSA_TK), jnp.float32),
            pltpu.VMEM((rows, DSA_TK), jnp.bfloat16), pltpu.VMEM((rows, DSA_TK), jnp.bfloat16),
            pltpu.VMEM((rows, LANES), jnp.float32), pltpu.VMEM((rows, LANES), jnp.float32),
        ],
        compiler_params=_cparams(("parallel", "parallel")),
    )(q_pad, kv, mask_t, bres)


def _mixer_out_mlp_kernel(h_ref, a_ref, wo_ref, g_ref, wu_ref, wd_ref, o_ref, *, tf):
    x = h_ref[...] + jnp.dot(a_ref[...], wo_ref[...], preferred_element_type=jnp.float32)
    xn = _rms(x, g_ref[...]).astype(jnp.bfloat16)
    o_ref[...] = x
    for f in range(0, D_FF, tf):
        u = jnp.dot(xn, wu_ref[:, f:f + tf], preferred_element_type=jnp.float32)
        a = jnp.square(jnp.maximum(u, 0.0)).astype(jnp.bfloat16)
        o_ref[...] += jnp.dot(a, wd_ref[f:f + tf, :], preferred_element_type=jnp.float32)


def _mixer_out_mlp(h2d, a2d, wo, g, wu, wd):
    n, d = h2d.shape
    tm = min(ROW_TILE, n)
    resident = lambda shape: pl.BlockSpec(shape, lambda i: (0, 0), pipeline_mode=pl.Buffered(1))
    return pl.pallas_call(
        functools.partial(_mixer_out_mlp_kernel, tf=1024),
        grid=(n // tm,),
        in_specs=[
            pl.BlockSpec((tm, d), lambda i: (i, 0)),
            pl.BlockSpec((tm, a2d.shape[1]), lambda i: (i, 0)),
            resident(wo.shape),
            pl.BlockSpec((1, d), lambda i: (0, 0)),
            resident(wu.shape), resident(wd.shape),
        ],
        out_specs=pl.BlockSpec((tm, d), lambda i: (i, 0)),
        out_shape=jax.ShapeDtypeStruct((n, d), jnp.float32),
        compiler_params=_cparams(("parallel",)),
    )(h2d, a2d, wo, g, wu, wd)


def _rope_padded(x, cos_f, sin_a, sin_b):
    return x * cos_f + pltpu.roll(x, LANES - QK_ROPE // 2, 1) * sin_a + pltpu.roll(x, QK_ROPE // 2, 1) * sin_b


def _mla_kv_kernel(h_ref, g_ref, wc_ref, wr_ref, gl_ref, wk_ref, wv_ref, gk_ref, cos_ref, sa_ref, sb_ref,
                   k_ref, v_ref):
    xn = _rms(h_ref[0], g_ref[...]).astype(jnp.bfloat16)
    c = jnp.dot(xn, wc_ref[...], preferred_element_type=jnp.float32)
    c = _rms(c, gl_ref[...]).astype(jnp.bfloat16)
    k_rope = jnp.dot(xn, wr_ref[...], preferred_element_type=jnp.float32)
    kn = jnp.dot(c, wk_ref[...], preferred_element_type=jnp.float32)
    vv = jnp.dot(c, wv_ref[...], preferred_element_type=jnp.float32)
    lane = lax.broadcasted_iota(jnp.int32, (1, LANES), 1)
    v_ones = jnp.where(lane == V_DIM, 1.0, 0.0)
    cos_f, sin_a, sin_b = cos_ref[...], sa_ref[...], sb_ref[...]
    for h in range(B_HEADS):
        kh = kn[:, h * LANES:(h + 1) * LANES] + k_rope
        kh = _rope_padded(_rms_padded(kh, gk_ref[...], QK_DIM), cos_f, sin_a, sin_b)
        k_ref[0, h] = kh.astype(jnp.bfloat16)
        v_ref[0, h] = (vv[:, h * LANES:(h + 1) * LANES] + v_ones).astype(jnp.bfloat16)


def _mla_kv_prep(h, g, wc, wr, gl, wk, wv, gk, cos_f, sin_a, sin_b):
    b, s, d = h.shape
    tm = min(ROW_TILE, s)
    const = lambda shape: pl.BlockSpec(shape, lambda i, j: (0,) * len(shape))
    pos = pl.BlockSpec((tm, LANES), lambda i, j: (j, 0))
    head_out = pl.BlockSpec((1, B_HEADS, tm, LANES), lambda i, j: (i, 0, j, 0))
    return pl.pallas_call(
        _mla_kv_kernel,
        grid=(b, s // tm),
        in_specs=[
            pl.BlockSpec((1, tm, d), lambda i, j: (i, j, 0)),
            const((1, d)), const(wc.shape), const(wr.shape), const((1, KV_LORA)), const(wk.shape),
            const(wv.shape), const((1, LANES)), pos, pos, pos,
        ],
        out_specs=[head_out, head_out],
        out_shape=[jax.ShapeDtypeStruct((b, B_HEADS, s, LANES), jnp.bfloat16)] * 2,
        compiler_params=_cparams(("parallel", "parallel")),
    )(h, g, wc, wr, gl, wk, wv, gk, cos_f, sin_a, sin_b)


def _mla_q_kernel(h_ref, g_ref, wd_ref, gl_ref, wu_ref, gq_ref, cos_ref, sa_ref, sb_ref, q_ref):
    xn = _rms(h_ref[0], g_ref[...]).astype(jnp.bfloat16)
    c = jnp.dot(xn, wd_ref[...], preferred_element_type=jnp.float32)
    c = _rms(c, gl_ref[...]).astype(jnp.bfloat16)
    yq = jnp.dot(c, wu_ref[...], preferred_element_type=jnp.float32)
    cos_f, sin_a, sin_b = cos_ref[...], sa_ref[...], sb_ref[...]
    for h in range(B_HEADS):
        qh = _rms_padded(yq[:, h * LANES:(h + 1) * LANES], gq_ref[...], QK_DIM)
        q_ref[0, h] = _rope_padded(qh, cos_f, sin_a, sin_b).astype(jnp.bfloat16)


def _mla_q_prep(h, g, wd, gl, wu, gq, cos_f, sin_a, sin_b):
    b, s, d = h.shape
    tm = min(ROW_TILE, s)
    const = lambda shape: pl.BlockSpec(shape, lambda i, j: (0,) * len(shape))
    pos = pl.BlockSpec((tm, LANES), lambda i, j: (j, 0))
    return pl.pallas_call(
        _mla_q_kernel,
        grid=(b, s // tm),
        in_specs=[
            pl.BlockSpec((1, tm, d), lambda i, j: (i, j, 0)),
            const((1, d)), const(wd.shape), const((1, Q_LORA)), const(wu.shape), const((1, LANES)),
            pos, pos, pos,
        ],
        out_specs=pl.BlockSpec((1, B_HEADS, tm, LANES), lambda i, j: (i, 0, j, 0)),
        out_shape=jax.ShapeDtypeStruct((b, B_HEADS, s, LANES), jnp.bfloat16),
        compiler_params=_cparams(("parallel", "parallel")),
    )(h, g, wd, gl, wu, gq, cos_f, sin_a, sin_b)


def _mla_attn_kernel(q_ref, k_ref, v_ref, o_ref, m_scr, acc_scr, s_0, s_1, p_0, p_1, al_0, al_1, *, tq):
    diag = pl.program_id(2)
    m_scr[...] = jnp.full(m_scr.shape, -jnp.inf, jnp.float32)
    acc_scr[...] = jnp.zeros(acc_scr.shape, jnp.float32)
    p_1[...] = jnp.zeros(p_1.shape, jnp.bfloat16)
    al_1[...] = jnp.ones(al_1.shape, jnp.float32)

    def qk_stage(j, hh, s_out):
        k0 = pl.multiple_of(j * tq, tq)
        s_out[...] = lax.dot_general(q_ref[0, hh], k_ref[0, hh, pl.ds(k0, tq), :], _NT,
                                     preferred_element_type=jnp.float32)

    def softmax_stage(hh, s_in, p_out, al_out, masked):
        for g0 in range(0, tq, MLA_ROW_GROUP):
            g1 = g0 + MLA_ROW_GROUP
            s = s_in[g0:g1, :]
            if masked:
                q_chunk = (g0 + lax.broadcasted_iota(jnp.int32, (MLA_ROW_GROUP, tq), 0)) >> CHUNK_SHIFT
                k_chunk = lax.broadcasted_iota(jnp.int32, (MLA_ROW_GROUP, tq), 1) >> CHUNK_SHIFT
                s = jnp.where(k_chunk <= q_chunk, s, NEG)
            m_old = m_scr[hh, g0:g1, :]
            m_new = jnp.maximum(m_old, jnp.max(s, axis=1, keepdims=True))
            al_out[g0:g1, :] = jnp.exp2(m_old - m_new)
            p_out[g0:g1, :] = jnp.exp2((s - jnp.concatenate([m_new] * (tq // LANES), axis=1)).astype(jnp.bfloat16))
            m_scr[hh, g0:g1, :] = m_new

    def pv_stage(j, hh, p_in, al_in):
        v0 = pl.multiple_of(jnp.maximum(j, 0) * tq, tq)
        pv = jnp.dot(p_in[...], v_ref[0, hh, pl.ds(v0, tq), :], preferred_element_type=jnp.float32)
        acc_scr[hh] = acc_scr[hh] * al_in[...] + pv

    qk_stage(0, 0, s_0)

    def full_tile(j, carry):
        qk_stage(j, 1, s_1)
        softmax_stage(0, s_0, p_0, al_0, False)
        pv_stage(j - 1, 1, p_1, al_1)
        qk_stage(j + 1, 0, s_0)
        softmax_stage(1, s_1, p_1, al_1, False)
        pv_stage(j, 0, p_0, al_0)
        return carry

    lax.fori_loop(0, diag, full_tile, 0)
    qk_stage(diag, 1, s_1)
    softmax_stage(0, s_0, p_0, al_0, True)
    pv_stage(diag - 1, 1, p_1, al_1)
    softmax_stage(1, s_1, p_1, al_1, True)
    pv_stage(diag, 0, p_0, al_0)
    pv_stage(diag, 1, p_1, al_1)
    outs = []
    for hh in range(2):
        a = acc_scr[hh]
        outs.append(a[:, :V_DIM] / a[:, V_DIM:V_DIM + 1])
    o_ref[0] = jnp.concatenate(outs, axis=1).astype(jnp.bfloat16)


def _mla_attn(q_pad, k_pad, v_aug):
    b, nh, s, _ = q_pad.shape
    tq = min(MLA_TQ, s)
    return pl.pallas_call(
        functools.partial(_mla_attn_kernel, tq=tq),
        grid=(b, nh // 2, s // tq),
        in_specs=[
            pl.BlockSpec((1, 2, tq, LANES), lambda i, p, j: (i, p, j, 0)),
            pl.BlockSpec((1, 2, s, LANES), lambda i, p, j: (i, p, 0, 0)),
            pl.BlockSpec((1, 2, s, LANES), lambda i, p, j: (i, p, 0, 0)),
        ],
        out_specs=pl.BlockSpec((1, tq, 2 * V_DIM), lambda i, p, j: (i, j, p)),
        out_shape=jax.ShapeDtypeStruct((b, s, nh * V_DIM), jnp.bfloat16),
        scratch_shapes=[
            pltpu.VMEM((2, tq, LANES), jnp.float32), pltpu.VMEM((2, tq, LANES), jnp.float32),
            pltpu.VMEM((tq, tq), jnp.float32), pltpu.VMEM((tq, tq), jnp.float32),
            pltpu.VMEM((tq, tq), jnp.bfloat16), pltpu.VMEM((tq, tq), jnp.bfloat16),
            pltpu.VMEM((tq, LANES), jnp.float32), pltpu.VMEM((tq, LANES), jnp.float32),
        ],
        compiler_params=_cparams(("parallel", "parallel", "parallel")),
    )(q_pad, k_pad, v_aug)


def _t5_bucket(rel):
    nb = N_BUCKETS // 2
    max_exact = nb // 2
    ret = jnp.where(rel > 0, nb, 0)
    n = jnp.abs(rel)
    nf = jnp.maximum(n, 1).astype(jnp.float32)
    large = max_exact + (jnp.log(nf / max_exact) / math.log(MAX_DISTANCE / max_exact)
                         * (nb - max_exact)).astype(jnp.int32)
    large = jnp.minimum(large, nb - 1)
    return ret + jnp.where(n < max_exact, n, large)


def _pad_lanes(a, width=LANES):
    return jnp.pad(a, [(0, 0)] * (a.ndim - 1) + [(0, width - a.shape[-1])])


def _pad_heads(w, n_heads, head_dim):
    k = w.shape[0]
    return _pad_lanes(w.reshape(k, n_heads, head_dim)).reshape(k, n_heads * LANES)


def _bias_tables(rel_bias):
    far = rel_bias[N_BUCKETS // 2 - 1].astype(jnp.float32) * LOG2E
    c_hi = far.astype(jnp.bfloat16)
    c_lo = (far - c_hi.astype(jnp.float32)).astype(jnp.bfloat16)
    far_eff = c_hi.astype(jnp.float32) + c_lo.astype(jnp.float32)
    cb = jnp.zeros((A_HEADS, LANES), jnp.float32)
    cb = cb.at[:, A_HEAD_DIM].set(c_hi.astype(jnp.float32)).at[:, A_HEAD_DIM + 1].set(c_lo.astype(jnp.float32))
    r = jnp.arange(DSA_TQ, dtype=jnp.int32)[:, None]
    c = jnp.arange(DSA_TK, dtype=jnp.int32)[None, :]
    table = rel_bias.astype(jnp.float32) * LOG2E - far_eff[None, :]
    tiles = []
    for off in (0, DSA_TQ, DSA_TK):
        onehot = jax.nn.one_hot(_t5_bucket(c - r - off), N_BUCKETS, dtype=jnp.float32)
        tiles.append(jnp.einsum('rcb,bh->hrc', onehot, table, precision=lax.Precision.HIGHEST))
    tiles.append(jnp.zeros_like(tiles[0]))
    bres = jnp.stack(tiles).reshape(4, A_HEADS * DSA_TQ, DSA_TK)
    return cb, bres


def _rope_tables_padded(seq_len):
    pos = jnp.arange(seq_len, dtype=jnp.float32)
    inv_freq = 1.0 / (ROPE_THETA ** (jnp.arange(0, QK_ROPE, 2, dtype=jnp.float32) / QK_ROPE))
    ang = pos[:, None] * inv_freq[None, :]
    cos, sin = jnp.cos(ang), jnp.sin(ang)
    half = QK_ROPE // 2
    ones = jnp.ones((seq_len, QK_NOPE), jnp.float32)
    zeros = jnp.zeros((seq_len, QK_NOPE), jnp.float32)
    zh = jnp.zeros((seq_len, half), jnp.float32)
    cos_f = _pad_lanes(jnp.concatenate([ones, cos, cos], axis=1))
    sin_a = _pad_lanes(jnp.concatenate([zeros, -sin, zh], axis=1))
    sin_b = _pad_lanes(jnp.concatenate([zeros, zh, sin], axis=1))
    return cos_f, sin_a, sin_b


def kernel(x, rel_bias, a_attn_norm, a_w_in, a_q_norm, a_k_norm, a_w_o, kv_norm, w_dkv, kv_lora_norm, w_ukv, k_norm, b_attn_norm, b_w_dq, b_q_lora_norm, b_w_uq, b_q_norm, b_w_o, mlp_norm, mlp_w_up, mlp_w_down):
    b, s, d = x.shape
    assert d == D_MODEL and s % DSA_TK == 0 and s % min(MLA_TQ, s) == 0
    n_top = min(TOPK_MAX, s // 4)
    bf = jnp.bfloat16
    n_a = a_w_in.shape[0]
    n_b = b_w_dq.shape[0]
    cb, bres = _bias_tables(rel_bias)
    cos_f, sin_a, sin_b = _rope_tables_padded(s)
    row = lambda v: v.reshape(1, -1).astype(jnp.float32)

    o1 = A_HEADS * A_HEAD_DIM
    o2 = o1 + A_HEAD_DIM
    o3 = o2 + A_HEAD_DIM
    o4 = o3 + IDX_HEADS * IDX_DIM
    o5 = o4 + IDX_DIM

    h = x
    layer = 0
    for i in range(n_a):
        w_in = a_w_in[i]
        wq = _pad_heads(w_in[:, :o1], A_HEADS, A_HEAD_DIM).astype(bf)
        wr = jnp.concatenate([_pad_lanes(w_in[:, o1:o2]), _pad_lanes(w_in[:, o2:o3]), w_in[:, o3:o4],
                              _pad_lanes(w_in[:, o4:])], axis=1).astype(bf)
        gq = _pad_lanes(row(a_q_norm[i]) * (A_HEAD_DIM ** -0.5 * LOG2E))
        gk = _pad_lanes(row(a_k_norm[i]))
        q_pad, kv, qi, misc = _dsa_proj(h, row(a_attn_norm[i]), wq, wr, gq, gk, cb)
        ki = misc[..., :IDX_DIM].astype(bf)
        wi_t = jnp.swapaxes(misc[..., IDX_DIM:IDX_DIM + IDX_HEADS] * (IDX_HEADS ** -0.5), 1, 2)
        mask_t = _dsa_select(ki, qi, wi_t, n_top)
        attn = _dsa_attn(q_pad, kv, mask_t, bres)
        h2 = _mixer_out_mlp(h.reshape(b * s, d), attn.reshape(b * s, -1), a_w_o[i].astype(bf),
                            row(mlp_norm[layer]), mlp_w_up[layer].astype(bf), mlp_w_down[layer].astype(bf))
        h = h2.reshape(b, s, d)
        layer += 1

    w_ukv3 = w_ukv.reshape(KV_LORA, B_HEADS, QK_NOPE + V_DIM)
    wk = _pad_lanes(w_ukv3[:, :, :QK_NOPE]).reshape(KV_LORA, B_HEADS * LANES).astype(bf)
    wv = _pad_lanes(w_ukv3[:, :, QK_NOPE:]).reshape(KV_LORA, B_HEADS * LANES).astype(bf)
    w_kr = jnp.pad(w_dkv[:, KV_LORA:], ((0, 0), (QK_NOPE, LANES - QK_DIM))).astype(bf)
    k_pad, v_aug = _mla_kv_prep(h, row(kv_norm), w_dkv[:, :KV_LORA].astype(bf), w_kr, row(kv_lora_norm),
                                wk, wv, _pad_lanes(row(k_norm)), cos_f, sin_a, sin_b)
    for j in range(n_b):
        wu = _pad_heads(b_w_uq[j], B_HEADS, QK_DIM).astype(bf)
        gq = _pad_lanes(row(b_q_norm[j]) * (QK_DIM ** -0.5 * LOG2E))
        q_pad = _mla_q_prep(h, row(b_attn_norm[j]), b_w_dq[j].astype(bf), row(b_q_lora_norm[j]), wu, gq,
                            cos_f, sin_a, sin_b)
        attn = _mla_attn(q_pad, k_pad, v_aug)
        h2 = _mixer_out_mlp(h.reshape(b * s, d), attn.reshape(b * s, -1), b_w_o[j].astype(bf),
                            row(mlp_norm[layer]), mlp_w_up[layer].astype(bf), mlp_w_down[layer].astype(bf))
        h = h2.reshape(b, s, d)
        layer += 1
    return h
```

```python
import functools
import math

import numpy as np
import jax
import jax.numpy as jnp
from jax import lax
from jax.experimental import pallas as pl
from jax.experimental.pallas import tpu as pltpu

D_MODEL = 1024
CHUNK = 64
CHUNK_SHIFT = 6
EPS = 1e-6
NEG = -1e30
LOG2E = 1.4426950408889634

A_HEADS = 16
A_HEAD_DIM = 64
IDX_HEADS = 8
IDX_DIM = 64
TOPK_MAX = 256
N_BUCKETS = 32
MAX_DISTANCE = 128

B_HEADS = 16
QK_NOPE = 64
QK_ROPE = 32
QK_DIM = QK_NOPE + QK_ROPE
V_DIM = 64
KV_LORA = 256
Q_LORA = 384
ROPE_THETA = 10000.0
D_FF = 4 * D_MODEL

LANES = 128
MIN_NORMAL_HI = 0x0080
SENTINEL_SCORE = -3.0e38
_SENTINEL_BITS = int(np.float32(SENTINEL_SCORE).view(np.int32))
SENTINEL_KEY = _SENTINEL_BITS ^ ((_SENTINEL_BITS >> 31) & 0x7FFFFFFF)
VMEM_LIMIT = 56 * 1024 * 1024

ROW_TILE = 512
DSA_TQ = 128
DSA_TK = 256
SEL_RB = 256
MLA_TQ = 512
MLA_ROW_GROUP = 128

_NT = (((1,), (1,)), ((), ()))


def _cparams(sem):
    return pltpu.CompilerParams(dimension_semantics=sem, vmem_limit_bytes=VMEM_LIMIT)


def _rms(x, g):
    return x * lax.rsqrt(jnp.mean(x * x, axis=-1, keepdims=True) + EPS) * g


def _rms_padded(x, g, true_dim):
    ms = jnp.sum(x * x, axis=-1, keepdims=True) * (1.0 / true_dim)
    return x * lax.rsqrt(ms + EPS) * g


def _dsa_proj_kernel(h_ref, g_ref, wq_ref, wr_ref, gq_ref, gk_ref, cb_ref, q_ref, kv_ref, qi_ref, misc_ref):
    xn = _rms(h_ref[0], g_ref[...]).astype(jnp.bfloat16)
    yq = jnp.dot(xn, wq_ref[...], preferred_element_type=jnp.float32)
    for h in range(A_HEADS):
        qh = yq[:, h * LANES:(h + 1) * LANES]
        qn = _rms_padded(qh, gq_ref[...], A_HEAD_DIM) + cb_ref[h:h + 1, :]
        q_ref[0, h] = qn.astype(jnp.bfloat16)
    yr = jnp.dot(xn, wr_ref[...], preferred_element_type=jnp.float32)
    lane = lax.broadcasted_iota(jnp.int32, (1, LANES), 1)
    k_ones = jnp.where((lane == A_HEAD_DIM) | (lane == A_HEAD_DIM + 1), 1.0, 0.0)
    v_ones = jnp.where(lane == A_HEAD_DIM, 1.0, 0.0)
    kn = _rms_padded(yr[:, :LANES], gk_ref[...], A_HEAD_DIM) + k_ones
    vv = yr[:, LANES:2 * LANES] + v_ones
    kv_ref[0, :, :LANES] = kn.astype(jnp.bfloat16)
    kv_ref[0, :, LANES:] = vv.astype(jnp.bfloat16)
    qi = yr[:, 2 * LANES:2 * LANES + IDX_HEADS * IDX_DIM] * (IDX_DIM ** -0.5)
    for h in range(IDX_HEADS):
        qi_ref[0, h] = qi[:, h * IDX_DIM:(h + 1) * IDX_DIM].astype(jnp.bfloat16)
    misc_ref[0] = yr[:, 2 * LANES + IDX_HEADS * IDX_DIM:]


def _dsa_proj(h, g, wq, wr, gq, gk, cb):
    b, s, d = h.shape
    tm = min(ROW_TILE, s)
    const = lambda shape: pl.BlockSpec(shape, lambda i, j: (0,) * len(shape))
    return pl.pallas_call(
        _dsa_proj_kernel,
        grid=(b, s // tm),
        in_specs=[
            pl.BlockSpec((1, tm, d), lambda i, j: (i, j, 0)),
            const((1, d)), const(wq.shape), const(wr.shape), const((1, LANES)), const((1, LANES)),
            const((A_HEADS, LANES)),
        ],
        out_specs=[
            pl.BlockSpec((1, A_HEADS, tm, LANES), lambda i, j: (i, 0, j, 0)),
            pl.BlockSpec((1, tm, 2 * LANES), lambda i, j: (i, j, 0)),
            pl.BlockSpec((1, IDX_HEADS, tm, IDX_DIM), lambda i, j: (i, 0, j, 0)),
            pl.BlockSpec((1, tm, LANES), lambda i, j: (i, j, 0)),
        ],
        out_shape=[
            jax.ShapeDtypeStruct((b, A_HEADS, s, LANES), jnp.bfloat16),
            jax.ShapeDtypeStruct((b, s, 2 * LANES), jnp.bfloat16),
            jax.ShapeDtypeStruct((b, IDX_HEADS, s, IDX_DIM), jnp.bfloat16),
            jax.ShapeDtypeStruct((b, s, LANES), jnp.float32),
        ],
        compiler_params=_cparams(("parallel", "parallel")),
    )(h, g, wq, wr, gq, gk, cb)


def _dsa_select_kernel(ki_ref, qi_ref, wi_ref, mask_ref, keys_scr, hi_scr, *, n_top, seq):
    rb = SEL_RB
    t0 = pl.program_id(1) * DSA_TQ
    nblk = (t0 + DSA_TQ + rb - 1) // rb
    qi2d = qi_ref[0].reshape(IDX_HEADS * DSA_TQ, IDX_DIM)
    w = wi_ref[0]
    t_chunk = (t0 + lax.broadcasted_iota(jnp.int32, (1, DSA_TQ), 1)) >> CHUNK_SHIFT
    row_iota = lax.broadcasted_iota(jnp.int32, (rb, DSA_TQ), 0)

    def admissible(r0):
        return ((r0 + row_iota) >> CHUNK_SHIFT) <= t_chunk

    def idx_dots(blk):
        r0 = pl.multiple_of(jnp.minimum(blk, seq // rb - 1) * rb, rb)
        return lax.dot_general(ki_ref[0, pl.ds(r0, rb), :], qi2d, _NT, preferred_element_type=jnp.float32)

    def store_keys(blk, d):
        r0 = pl.multiple_of(blk * rb, rb)
        acc = jnp.zeros((rb, DSA_TQ), jnp.float32)
        for h in range(IDX_HEADS):
            acc = acc + w[h:h + 1, :] * jnp.maximum(d[:, h * DSA_TQ:(h + 1) * DSA_TQ], 0.0)
        bits = lax.bitcast_convert_type(jnp.where(admissible(r0), acc, SENTINEL_SCORE), jnp.int32)
        keys_scr[pl.ds(r0, rb), :] = bits ^ ((bits >> 31) & 0x7FFFFFFF)
        hi_scr[pl.ds(r0, rb), :] = lax.bitcast_convert_type(bits & -65536, jnp.float32).astype(jnp.bfloat16)

    def score_pair(pp, carry):
        d0 = idx_dots(2 * pp)
        d1 = idx_dots(2 * pp + 1)
        store_keys(2 * pp, d0)
        store_keys(2 * pp + 1, d1)
        return carry

    lax.fori_loop(0, (nblk + 1) // 2, score_pair, 0)

    cb = 2 * rb
    n_cnt = (nblk + 1) // 2

    def count(pred_fn):
        def blk_body(step, cnt):
            r0 = pl.multiple_of(step * cb, cb)
            hit = jnp.where(pred_fn(keys_scr[pl.ds(r0, cb), :]), 1, 0)
            return cnt + jnp.sum(hit.reshape(cb // 8, 8, DSA_TQ), axis=0)
        cnt8 = lax.fori_loop(0, n_cnt, blk_body, jnp.zeros((8, DSA_TQ), jnp.int32))
        return jnp.sum(cnt8, axis=0, keepdims=True)

    def count_hi(trial_b):
        one, zero = jnp.ones((), jnp.bfloat16), jnp.zeros((), jnp.bfloat16)

        def blk_body(step, cnt):
            r0 = pl.multiple_of(step * cb, cb)
            hit = jnp.where(hi_scr[pl.ds(r0, cb), :] >= trial_b, one, zero)
            parts = [hit[r:r + 16, :] for r in range(0, cb, 16)]
            while len(parts) > 1:
                parts = [parts[i] + parts[i + 1] for i in range(0, len(parts), 2)]
            return cnt + parts[0].astype(jnp.float32)
        cnt16 = lax.fori_loop(0, n_cnt, blk_body, jnp.zeros((16, DSA_TQ), jnp.float32))
        return jnp.sum(cnt16, axis=0, keepdims=True)

    def hi_pass(i, carry):
        thr_u, cnt_thr = carry
        trial_u = thr_u + lax.shift_left(jnp.int32(1), 15 - i)
        k = trial_u - 32768
        k = jnp.where((k > 0) & (k < MIN_NORMAL_HI), MIN_NORMAL_HI, k)
        trial_bits = lax.shift_left(k ^ ((k >> 31) & 0x7FFF), 16)
        trial_b = lax.bitcast_convert_type(trial_bits, jnp.float32).astype(jnp.bfloat16)
        cnt = count_hi(trial_b)
        ok = cnt >= n_top
        return jnp.where(ok, trial_u, thr_u), jnp.where(ok, cnt, cnt_thr)

    thr_u, cnt_thr = lax.fori_loop(0, 16, hi_pass, (jnp.zeros((1, DSA_TQ), jnp.int32),
                                                    jnp.full((1, DSA_TQ), n_top, jnp.float32)))

    def lo_pass(i, carry):
        thr, cnt_thr = carry
        trial = thr + lax.shift_left(jnp.int32(1), 15 - i)
        cnt = count(lambda k: k >= trial).astype(jnp.float32)
        ok = cnt >= n_top
        return jnp.where(ok, trial, thr), jnp.where(ok, cnt, cnt_thr)

    thr, cnt_thr = lax.fori_loop(0, 16, lo_pass, (lax.shift_left(thr_u - 32768, 16), cnt_thr))
    has_ties = jnp.max(jnp.where((cnt_thr != n_top) & (thr != SENTINEL_KEY), 1, 0)) > 0

    @pl.when(jnp.logical_not(has_ties))
    def _():
        def emit_blk(blk, carry):
            r0 = pl.multiple_of(blk * rb, rb)
            sel = (keys_scr[pl.ds(r0, rb), :] >= thr) & admissible(r0)
            mask_ref[0, pl.ds(r0, rb), :] = jnp.where(sel, 0.0, NEG).astype(jnp.bfloat16)
            return carry

        lax.fori_loop(0, nblk, emit_blk, 0)

    @pl.when(has_ties)
    def _():
        quota = (n_top - count(lambda k: k > thr)).astype(jnp.float32)
        tri = (lax.broadcasted_iota(jnp.int32, (rb, rb), 1)
               < lax.broadcasted_iota(jnp.int32, (rb, rb), 0))
        tri = jnp.where(tri, 1.0, 0.0).astype(jnp.bfloat16)

        def emit_blk(blk, seen):
            r0 = pl.multiple_of(blk * rb, rb)
            key = keys_scr[pl.ds(r0, rb), :]
            eq = key == thr
            eqf = jnp.where(eq, 1.0, 0.0)
            before = jnp.dot(tri, eqf.astype(jnp.bfloat16), preferred_element_type=jnp.float32) + seen
            sel = (key > thr) | (eq & (before < quota))
            sel = sel & admissible(r0)
            mask_ref[0, pl.ds(r0, rb), :] = jnp.where(sel, 0.0, NEG).astype(jnp.bfloat16)
            return seen + jnp.sum(eqf, axis=0, keepdims=True)

        lax.fori_loop(0, nblk, emit_blk, jnp.zeros((1, DSA_TQ), jnp.float32))

    def fill_blk(blk, carry):
        r0 = pl.multiple_of(blk * rb, rb)
        mask_ref[0, pl.ds(r0, rb), :] = jnp.full((rb, DSA_TQ), NEG, jnp.bfloat16)
        return carry

    lax.fori_loop(nblk, (seq + DSA_TK) // rb, fill_blk, 0)


def _dsa_select(ki, qi, wi_t, n_top):
    b, s, _ = ki.shape
    return pl.pallas_call(
        functools.partial(_dsa_select_kernel, n_top=n_top, seq=s),
        grid=(b, s // DSA_TQ),
        in_specs=[
            pl.BlockSpec((1, s, IDX_DIM), lambda i, j: (i, 0, 0)),
            pl.BlockSpec((1, IDX_HEADS, DSA_TQ, IDX_DIM), lambda i, j: (i, 0, j, 0)),
            pl.BlockSpec((1, IDX_HEADS, DSA_TQ), lambda i, j: (i, 0, j)),
        ],
        out_specs=pl.BlockSpec((1, s + DSA_TK, DSA_TQ), lambda i, j: (i, 0, j)),
        out_shape=jax.ShapeDtypeStruct((b, s + DSA_TK, s), jnp.bfloat16),
        scratch_shapes=[pltpu.VMEM((s + SEL_RB, DSA_TQ), jnp.int32),
                        pltpu.VMEM((s + SEL_RB, DSA_TQ), jnp.bfloat16)],
        compiler_params=_cparams(("parallel", "parallel")),
    )(ki, qi, wi_t)


def _dsa_attn_kernel(q_ref, kv_ref, mask_ref, bres_ref, o_ref,
                     lhs_scr, m_scr, acc_scr, s_a, s_b, p_a, p_b, al_a, al_b, *, n_tiles):
    rows = A_HEADS * DSA_TQ
    qidx = pl.program_id(1)
    nk = (qidx * DSA_TQ + DSA_TQ + DSA_TK - 1) // DSA_TK
    odd = qidx % 2
    n_pairs = (nk + 1) // 2

    lhs_scr[:, :LANES] = q_ref[0].reshape(rows, LANES)
    r = lax.broadcasted_iota(jnp.int32, (rows, DSA_TQ), 0)
    c = lax.broadcasted_iota(jnp.int32, (rows, DSA_TQ), 1)
    lhs_scr[:, LANES:] = jnp.where((r & (DSA_TQ - 1)) == c, 1.0, 0.0).astype(jnp.bfloat16)
    m_scr[...] = jnp.full(m_scr.shape, -jnp.inf, jnp.float32)
    acc_scr[...] = jnp.zeros(acc_scr.shape, jnp.float32)
    p_b[...] = jnp.zeros(p_b.shape, jnp.bfloat16)
    al_b[...] = jnp.ones(al_b.shape, jnp.float32)

    def qk_stage(t, s_out):
        k0 = pl.multiple_of(jnp.minimum(t, n_tiles - 1) * DSA_TK, DSA_TK)
        m0 = pl.multiple_of(t * DSA_TK, DSA_TK)
        rhs = jnp.concatenate([kv_ref[0, pl.ds(k0, DSA_TK), :LANES], mask_ref[0, pl.ds(m0, DSA_TK), :]], axis=1)
        s_out[...] = lax.dot_general(lhs_scr[...], rhs, _NT, preferred_element_type=jnp.float32)

    def softmax_stage(t, s_in, p_out, al_out, near):
        cfg = jnp.where(t == nk - 1, odd, jnp.where((t == nk - 2) & (odd == 0), 2, 3))
        for h in range(A_HEADS):
            r0 = h * DSA_TQ
            s = s_in[r0:r0 + DSA_TQ, :]
            if near:
                s = s + bres_ref[cfg, r0:r0 + DSA_TQ, :]
            m_old = m_scr[r0:r0 + DSA_TQ, :]
            m_new = jnp.maximum(m_old, jnp.max(s, axis=1, keepdims=True))
            al_out[r0:r0 + DSA_TQ, :] = jnp.exp2(m_old - m_new)
            p_out[r0:r0 + DSA_TQ, :] = jnp.exp2((s - jnp.concatenate([m_new, m_new], axis=1)).astype(jnp.bfloat16))
            m_scr[r0:r0 + DSA_TQ, :] = m_new

    def pv_stage(t, p_in, al_in):
        v0 = pl.multiple_of(jnp.clip(t, 0, n_tiles - 1) * DSA_TK, DSA_TK)
        pv = jnp.dot(p_in[...], kv_ref[0, pl.ds(v0, DSA_TK), LANES:], preferred_element_type=jnp.float32)
        acc_scr[...] = acc_scr[...] * al_in[...] + pv

    qk_stage(0, s_a)

    def pair_body(jj, carry, near):
        t = 2 * jj
        qk_stage(t + 1, s_b)
        softmax_stage(t, s_a, p_a, al_a, near)
        pv_stage(t - 1, p_b, al_b)
        qk_stage(t + 2, s_a)
        softmax_stage(t + 1, s_b, p_b, al_b, near)
        pv_stage(t, p_a, al_a)
        return carry

    n_far = jnp.maximum(nk - 2, 0) // 2
    lax.fori_loop(0, n_far, functools.partial(pair_body, near=False), 0)
    lax.fori_loop(n_far, n_pairs, functools.partial(pair_body, near=True), 0)
    pv_stage(2 * n_pairs - 1, p_b, al_b)

    for h in range(A_HEADS):
        a = acc_scr[h * DSA_TQ:(h + 1) * DSA_TQ, :]
        o = a[:, :A_HEAD_DIM] / a[:, A_HEAD_DIM:A_HEAD_DIM + 1]
        o_ref[0, :, h * A_HEAD_DIM:(h + 1) * A_HEAD_DIM] = o.astype(jnp.bfloat16)


def _dsa_attn(q_pad, kv, mask_t, bres):
    b, _, s, _ = q_pad.shape
    rows = A_HEADS * DSA_TQ
    return pl.pallas_call(
        functools.partial(_dsa_attn_kernel, n_tiles=s // DSA_TK),
        grid=(b, s // DSA_TQ),
        in_specs=[
            pl.BlockSpec((1, A_HEADS, DSA_TQ, LANES), lambda i, j: (i, 0, j, 0)),
            pl.BlockSpec((1, s, 2 * LANES), lambda i, j: (i, 0, 0)),
            pl.BlockSpec((1, s + DSA_TK, DSA_TQ), lambda i, j: (i, 0, j)),
            pl.BlockSpec(bres.shape, lambda i, j: (0, 0, 0)),
        ],
        out_specs=pl.BlockSpec((1, DSA_TQ, A_HEADS * A_HEAD_DIM), lambda i, j: (i, j, 0)),
        out_shape=jax.ShapeDtypeStruct((b, s, A_HEADS * A_HEAD_DIM), jnp.bfloat16),
        scratch_shapes=[
            pltpu.VMEM((rows, 2 * LANES), jnp.bfloat16),
            pltpu.VMEM((rows, LANES), jnp.float32),
            pltpu.VMEM((rows, LANES), jnp.float32),
            pltpu.VMEM((rows, DSA_TK), jnp.float32), pltpu.VMEM((rows, DSA_TK), jnp.float32),
            pltpu.VMEM((rows, DSA_TK), jnp.bfloat16), pltpu.VMEM((rows, DSA_TK), jnp.bfloat16),
            pltpu.VMEM((rows, LANES), jnp.float32), pltpu.VMEM((rows, LANES), jnp.float32),
        ],
        compiler_params=_cparams(("parallel", "parallel")),
    )(q_pad, kv, mask_t, bres)


def _mixer_out_mlp_kernel(h_ref, a_ref, wo_ref, g_ref, wu_ref, wd_ref, o_ref, *, tf):
    x = h_ref[...] + jnp.dot(a_ref[...], wo_ref[...], preferred_element_type=jnp.float32)
    xn = _rms(x, g_ref[...]).astype(jnp.bfloat16)
    o_ref[...] = x
    for f in range(0, D_FF, tf):
        u = jnp.dot(xn, wu_ref[:, f:f + tf], preferred_element_type=jnp.float32)
        a = jnp.square(jnp.maximum(u, 0.0)).astype(jnp.bfloat16)
        o_ref[...] += jnp.dot(a, wd_ref[f:f + tf, :], preferred_element_type=jnp.float32)


def _mixer_out_mlp(h2d, a2d, wo, g, wu, wd):
    n, d = h2d.shape
    tm = min(ROW_TILE, n)
    resident = lambda shape: pl.BlockSpec(shape, lambda i: (0, 0), pipeline_mode=pl.Buffered(1))
    return pl.pallas_call(
        functools.partial(_mixer_out_mlp_kernel, tf=1024),
        grid=(n // tm,),
        in_specs=[
            pl.BlockSpec((tm, d), lambda i: (i, 0)),
            pl.BlockSpec((tm, a2d.shape[1]), lambda i: (i, 0)),
            resident(wo.shape),
            pl.BlockSpec((1, d), lambda i: (0, 0)),
            resident(wu.shape), resident(wd.shape),
        ],
        out_specs=pl.BlockSpec((tm, d), lambda i: (i, 0)),
        out_shape=jax.ShapeDtypeStruct((n, d), jnp.float32),
        compiler_params=_cparams(("parallel",)),
    )(h2d, a2d, wo, g, wu, wd)


def _rms_rope_padded(y, y_rot, cos_g, sin_g):
    ms = jnp.sum(y * y, axis=-1, keepdims=True) * (1.0 / QK_DIM)
    return (y * cos_g + y_rot * sin_g) * lax.rsqrt(ms + EPS)


def _mla_kv_kernel(h_ref, g_ref, wc_ref, wr_ref, wrr_ref, gl_ref, wk_ref, wv_ref, cos_ref, sin_ref, k_ref, v_ref):
    xn = _rms(h_ref[0], g_ref[...]).astype(jnp.bfloat16)
    c = jnp.dot(xn, wc_ref[...], preferred_element_type=jnp.float32)
    c = _rms(c, gl_ref[...]).astype(jnp.bfloat16)
    k_rope = jnp.dot(xn, wr_ref[...], preferred_element_type=jnp.float32)
    k_rot = jnp.dot(xn, wrr_ref[...], preferred_element_type=jnp.float32)
    kn = jnp.dot(c, wk_ref[...], preferred_element_type=jnp.float32)
    vv = jnp.dot(c, wv_ref[...], preferred_element_type=jnp.float32)
    lane = lax.broadcasted_iota(jnp.int32, (1, LANES), 1)
    v_ones = jnp.where(lane == V_DIM, 1.0, 0.0)
    cos_g, sin_g = cos_ref[...], sin_ref[...]
    for h in range(B_HEADS):
        kh = kn[:, h * LANES:(h + 1) * LANES] + k_rope
        k_ref[0, h] = _rms_rope_padded(kh, k_rot, cos_g, sin_g).astype(jnp.bfloat16)
        v_ref[0, h] = (vv[:, h * LANES:(h + 1) * LANES] + v_ones).astype(jnp.bfloat16)


def _mla_kv_prep(h, g, wc, wr, wrr, gl, wk, wv, cos_g, sin_g):
    b, s, d = h.shape
    tm = min(ROW_TILE, s)
    const = lambda shape: pl.BlockSpec(shape, lambda i, j: (0,) * len(shape))
    pos = pl.BlockSpec((tm, LANES), lambda i, j: (j, 0))
    head_out = pl.BlockSpec((1, B_HEADS, tm, LANES), lambda i, j: (i, 0, j, 0))
    return pl.pallas_call(
        _mla_kv_kernel,
        grid=(b, s // tm),
        in_specs=[
            pl.BlockSpec((1, tm, d), lambda i, j: (i, j, 0)),
            const((1, d)), const(wc.shape), const(wr.shape), const(wrr.shape), const((1, KV_LORA)),
            const(wk.shape), const(wv.shape), pos, pos,
        ],
        out_specs=[head_out, head_out],
        out_shape=[jax.ShapeDtypeStruct((b, B_HEADS, s, LANES), jnp.bfloat16)] * 2,
        compiler_params=_cparams(("parallel", "parallel")),
    )(h, g, wc, wr, wrr, gl, wk, wv, cos_g, sin_g)


def _mla_q_kernel(h_ref, g_ref, wd_ref, gl_ref, wu_ref, wur_ref, cos_ref, sin_ref, q_ref):
    xn = _rms(h_ref[0], g_ref[...]).astype(jnp.bfloat16)
    c = jnp.dot(xn, wd_ref[...], preferred_element_type=jnp.float32)
    c = _rms(c, gl_ref[...]).astype(jnp.bfloat16)
    yq = jnp.dot(c, wu_ref[...], preferred_element_type=jnp.float32)
    yr = jnp.dot(c, wur_ref[...], preferred_element_type=jnp.float32)
    cos_g, sin_g = cos_ref[...], sin_ref[...]
    for h in range(B_HEADS):
        hs = slice(h * LANES, (h + 1) * LANES)
        q_ref[0, h] = _rms_rope_padded(yq[:, hs], yr[:, hs], cos_g, sin_g).astype(jnp.bfloat16)


def _mla_q_prep(h, g, wd, gl, wu, wur, cos_g, sin_g):
    b, s, d = h.shape
    tm = min(ROW_TILE, s)
    const = lambda shape: pl.BlockSpec(shape, lambda i, j: (0,) * len(shape))
    pos = pl.BlockSpec((tm, LANES), lambda i, j: (j, 0))
    return pl.pallas_call(
        _mla_q_kernel,
        grid=(b, s // tm),
        in_specs=[
            pl.BlockSpec((1, tm, d), lambda i, j: (i, j, 0)),
            const((1, d)), const(wd.shape), const((1, Q_LORA)), const(wu.shape), const(wur.shape), pos, pos,
        ],
        out_specs=pl.BlockSpec((1, B_HEADS, tm, LANES), lambda i, j: (i, 0, j, 0)),
        out_shape=jax.ShapeDtypeStruct((b, B_HEADS, s, LANES), jnp.bfloat16),
        compiler_params=_cparams(("parallel", "parallel")),
    )(h, g, wd, gl, wu, wur, cos_g, sin_g)


def _mla_attn_kernel(q_ref, k_ref, v_ref, o_ref, m_scr, acc_scr, s_0, s_1, p_0, p_1, al_0, al_1, *, tq):
    diag = pl.program_id(2)
    m_scr[...] = jnp.full(m_scr.shape, -jnp.inf, jnp.float32)
    acc_scr[...] = jnp.zeros(acc_scr.shape, jnp.float32)
    p_1[...] = jnp.zeros(p_1.shape, jnp.bfloat16)
    al_1[...] = jnp.ones(al_1.shape, jnp.float32)

    def qk_stage(j, hh, s_out):
        k0 = pl.multiple_of(j * tq, tq)
        s_out[...] = lax.dot_general(q_ref[0, hh], k_ref[0, hh, pl.ds(k0, tq), :], _NT,
                                     preferred_element_type=jnp.float32)

    def softmax_stage(hh, s_in, p_out, al_out, masked):
        for g0 in range(0, tq, MLA_ROW_GROUP):
            g1 = g0 + MLA_ROW_GROUP
            s = s_in[g0:g1, :]
            if masked:
                q_chunk = (g0 + lax.broadcasted_iota(jnp.int32, (MLA_ROW_GROUP, tq), 0)) >> CHUNK_SHIFT
                k_chunk = lax.broadcasted_iota(jnp.int32, (MLA_ROW_GROUP, tq), 1) >> CHUNK_SHIFT
                s = jnp.where(k_chunk <= q_chunk, s, NEG)
            m_old = m_scr[hh, g0:g1, :]
            m_new = jnp.maximum(m_old, jnp.max(s, axis=1, keepdims=True))
            al_out[g0:g1, :] = jnp.exp2(m_old - m_new)
            p_out[g0:g1, :] = jnp.exp2((s - jnp.concatenate([m_new] * (tq // LANES), axis=1)).astype(jnp.bfloat16))
            m_scr[hh, g0:g1, :] = m_new

    def pv_stage(j, hh, p_in, al_in):
        v0 = pl.multiple_of(jnp.maximum(j, 0) * tq, tq)
        pv = jnp.dot(p_in[...], v_ref[0, hh, pl.ds(v0, tq), :], preferred_element_type=jnp.float32)
        acc_scr[hh] = acc_scr[hh] * al_in[...] + pv

    qk_stage(0, 0, s_0)

    def full_tile(j, carry):
        qk_stage(j, 1, s_1)
        softmax_stage(0, s_0, p_0, al_0, False)
        pv_stage(j - 1, 1, p_1, al_1)
        qk_stage(j + 1, 0, s_0)
        softmax_stage(1, s_1, p_1, al_1, False)
        pv_stage(j, 0, p_0, al_0)
        return carry

    lax.fori_loop(0, diag, full_tile, 0)
    qk_stage(diag, 1, s_1)
    softmax_stage(0, s_0, p_0, al_0, True)
    pv_stage(diag - 1, 1, p_1, al_1)
    softmax_stage(1, s_1, p_1, al_1, True)
    pv_stage(diag, 0, p_0, al_0)
    pv_stage(diag, 1, p_1, al_1)
    outs = []
    for hh in range(2):
        a = acc_scr[hh]
        outs.append(a[:, :V_DIM] / a[:, V_DIM:V_DIM + 1])
    o_ref[0] = jnp.concatenate(outs, axis=1).astype(jnp.bfloat16)


def _mla_attn(q_pad, k_pad, v_aug):
    b, nh, s, _ = q_pad.shape
    tq = min(MLA_TQ, s)
    return pl.pallas_call(
        functools.partial(_mla_attn_kernel, tq=tq),
        grid=(b, nh // 2, s // tq),
        in_specs=[
            pl.BlockSpec((1, 2, tq, LANES), lambda i, p, j: (i, p, j, 0)),
            pl.BlockSpec((1, 2, s, LANES), lambda i, p, j: (i, p, 0, 0)),
            pl.BlockSpec((1, 2, s, LANES), lambda i, p, j: (i, p, 0, 0)),
        ],
        out_specs=pl.BlockSpec((1, tq, 2 * V_DIM), lambda i, p, j: (i, j, p)),
        out_shape=jax.ShapeDtypeStruct((b, s, nh * V_DIM), jnp.bfloat16),
        scratch_shapes=[
            pltpu.VMEM((2, tq, LANES), jnp.float32), pltpu.VMEM((2, tq, LANES), jnp.float32),
            pltpu.VMEM((tq, tq), jnp.float32), pltpu.VMEM((tq, tq), jnp.float32),
            pltpu.VMEM((tq, tq), jnp.bfloat16), pltpu.VMEM((tq, tq), jnp.bfloat16),
            pltpu.VMEM((tq, LANES), jnp.float32), pltpu.VMEM((tq, LANES), jnp.float32),
        ],
        compiler_params=_cparams(("parallel", "parallel", "parallel")),
    )(q_pad, k_pad, v_aug)


def _t5_bucket(rel):
    nb = N_BUCKETS // 2
    max_exact = nb // 2
    ret = jnp.where(rel > 0, nb, 0)
    n = jnp.abs(rel)
    nf = jnp.maximum(n, 1).astype(jnp.float32)
    large = max_exact + (jnp.log(nf / max_exact) / math.log(MAX_DISTANCE / max_exact)
                         * (nb - max_exact)).astype(jnp.int32)
    large = jnp.minimum(large, nb - 1)
    return ret + jnp.where(n < max_exact, n, large)


def _pad_lanes(a, width=LANES):
    return jnp.pad(a, [(0, 0)] * (a.ndim - 1) + [(0, width - a.shape[-1])])


def _pad_heads(w, n_heads, head_dim):
    k = w.shape[0]
    return _pad_lanes(w.reshape(k, n_heads, head_dim)).reshape(k, n_heads * LANES)


def _bias_tables(rel_bias):
    far = rel_bias[N_BUCKETS // 2 - 1].astype(jnp.float32) * LOG2E
    c_hi = far.astype(jnp.bfloat16)
    c_lo = (far - c_hi.astype(jnp.float32)).astype(jnp.bfloat16)
    far_eff = c_hi.astype(jnp.float32) + c_lo.astype(jnp.float32)
    cb = jnp.zeros((A_HEADS, LANES), jnp.float32)
    cb = cb.at[:, A_HEAD_DIM].set(c_hi.astype(jnp.float32)).at[:, A_HEAD_DIM + 1].set(c_lo.astype(jnp.float32))
    r = jnp.arange(DSA_TQ, dtype=jnp.int32)[:, None]
    c = jnp.arange(DSA_TK, dtype=jnp.int32)[None, :]
    table = rel_bias.astype(jnp.float32) * LOG2E - far_eff[None, :]
    tiles = []
    for off in (0, DSA_TQ, DSA_TK):
        onehot = jax.nn.one_hot(_t5_bucket(c - r - off), N_BUCKETS, dtype=jnp.float32)
        tiles.append(jnp.einsum('rcb,bh->hrc', onehot, table, precision=lax.Precision.HIGHEST))
    tiles.append(jnp.zeros_like(tiles[0]))
    bres = jnp.stack(tiles).reshape(4, A_HEADS * DSA_TQ, DSA_TK)
    return cb, bres


def _rot_half_source():
    lane = np.arange(LANES)
    half = QK_ROPE // 2
    src, sign = lane.copy(), np.zeros(LANES, np.float32)
    lo = (lane >= QK_NOPE) & (lane < QK_NOPE + half)
    hi = (lane >= QK_NOPE + half) & (lane < QK_DIM)
    src[lo], sign[lo] = lane[lo] + half, -1.0
    src[hi], sign[hi] = lane[hi] - half, 1.0
    return src, sign


def _rot_half_columns(w_pad):
    src, sign = _rot_half_source()
    n = w_pad.shape[1] // LANES
    idx = (np.arange(n)[:, None] * LANES + src[None, :]).reshape(-1)
    return w_pad[:, idx] * jnp.asarray(np.tile(sign, n))


def _rope_gain_tables(seq_len, gain_pad):
    pos = jnp.arange(seq_len, dtype=jnp.float32)
    inv_freq = 1.0 / (ROPE_THETA ** (jnp.arange(0, QK_ROPE, 2, dtype=jnp.float32) / QK_ROPE))
    ang = pos[:, None] * inv_freq[None, :]
    cos, sin = jnp.cos(ang), jnp.sin(ang)
    ones = jnp.ones((seq_len, QK_NOPE), jnp.float32)
    zeros = jnp.zeros((seq_len, QK_NOPE), jnp.float32)
    cos_f = _pad_lanes(jnp.concatenate([ones, cos, cos], axis=1))
    sin_f = _pad_lanes(jnp.concatenate([zeros, sin, sin], axis=1))
    src, _ = _rot_half_source()
    return cos_f * gain_pad, sin_f * gain_pad[:, src]


def kernel(x, rel_bias, a_attn_norm, a_w_in, a_q_norm, a_k_norm, a_w_o, kv_norm, w_dkv, kv_lora_norm, w_ukv, k_norm, b_attn_norm, b_w_dq, b_q_lora_norm, b_w_uq, b_q_norm, b_w_o, mlp_norm, mlp_w_up, mlp_w_down):
    b, s, d = x.shape
    assert d == D_MODEL and s % DSA_TK == 0 and s % min(MLA_TQ, s) == 0
    n_top = min(TOPK_MAX, s // 4)
    bf = jnp.bfloat16
    n_a = a_w_in.shape[0]
    n_b = b_w_dq.shape[0]
    cb, bres = _bias_tables(rel_bias)
    row = lambda v: v.reshape(1, -1).astype(jnp.float32)

    o1 = A_HEADS * A_HEAD_DIM
    o2 = o1 + A_HEAD_DIM
    o3 = o2 + A_HEAD_DIM
    o4 = o3 + IDX_HEADS * IDX_DIM
    o5 = o4 + IDX_DIM

    h = x
    layer = 0
    for i in range(n_a):
        w_in = a_w_in[i]
        wq = _pad_heads(w_in[:, :o1], A_HEADS, A_HEAD_DIM).astype(bf)
        wr = jnp.concatenate([_pad_lanes(w_in[:, o1:o2]), _pad_lanes(w_in[:, o2:o3]), w_in[:, o3:o4],
                              _pad_lanes(w_in[:, o4:])], axis=1).astype(bf)
        gq = _pad_lanes(row(a_q_norm[i]) * (A_HEAD_DIM ** -0.5 * LOG2E))
        gk = _pad_lanes(row(a_k_norm[i]))
        q_pad, kv, qi, misc = _dsa_proj(h, row(a_attn_norm[i]), wq, wr, gq, gk, cb)
        ki = misc[..., :IDX_DIM].astype(bf)
        wi_t = jnp.swapaxes(misc[..., IDX_DIM:IDX_DIM + IDX_HEADS] * (IDX_HEADS ** -0.5), 1, 2)
        mask_t = _dsa_select(ki, qi, wi_t, n_top)
        attn = _dsa_attn(q_pad, kv, mask_t, bres)
        h2 = _mixer_out_mlp(h.reshape(b * s, d), attn.reshape(b * s, -1), a_w_o[i].astype(bf),
                            row(mlp_norm[layer]), mlp_w_up[layer].astype(bf), mlp_w_down[layer].astype(bf))
        h = h2.reshape(b, s, d)
        layer += 1

    w_ukv3 = w_ukv.reshape(KV_LORA, B_HEADS, QK_NOPE + V_DIM)
    wk = _pad_lanes(w_ukv3[:, :, :QK_NOPE]).reshape(KV_LORA, B_HEADS * LANES).astype(bf)
    wv = _pad_lanes(w_ukv3[:, :, QK_NOPE:]).reshape(KV_LORA, B_HEADS * LANES).astype(bf)
    w_kr = jnp.pad(w_dkv[:, KV_LORA:], ((0, 0), (QK_NOPE, LANES - QK_DIM)))
    cos_g, sin_g = _rope_gain_tables(s, _pad_lanes(row(k_norm)))
    k_pad, v_aug = _mla_kv_prep(h, row(kv_norm), w_dkv[:, :KV_LORA].astype(bf), w_kr.astype(bf),
                                _rot_half_columns(w_kr).astype(bf), row(kv_lora_norm), wk, wv, cos_g, sin_g)
    for j in range(n_b):
        wu = _pad_heads(b_w_uq[j], B_HEADS, QK_DIM)
        cos_g, sin_g = _rope_gain_tables(s, _pad_lanes(row(b_q_norm[j]) * (QK_DIM ** -0.5 * LOG2E)))
        q_pad = _mla_q_prep(h, row(b_attn_norm[j]), b_w_dq[j].astype(bf), row(b_q_lora_norm[j]), wu.astype(bf),
                            _rot_half_columns(wu).astype(bf), cos_g, sin_g)
        attn = _mla_attn(q_pad, k_pad, v_aug)
        h2 = _mixer_out_mlp(h.reshape(b * s, d), attn.reshape(b * s, -1), b_w_o[j].astype(bf),
                            row(mlp_norm[layer]), mlp_w_up[layer].astype(bf), mlp_w_down[layer].astype(bf))
        h = h2.reshape(b, s, d)
        layer += 1
    return h
```
